```python
import math
import jax, jax.numpy as jnp
from jax import lax
import numpy as np

D_MODEL = 4096
BATCH = 4
SEQ = 2048
DEPTH = 1

HEAD_DIM = 128
N_A_HEADS = 16
N_A_KV_HEADS = 4
N_A_GROUP = N_A_HEADS // N_A_KV_HEADS
B_PATTERNS = ((128, 1), (512, 4), (2048, 16))
N_B_GROUPS = len(B_PATTERNS)
N_B_HEADS_PER_GROUP = 4
N_B_HEADS = N_B_GROUPS * N_B_HEADS_PER_GROUP
N_BRANCHES = 2
Q_BLOCK = 128
GRID_W = 64
ROPE_THETA = 10000.0
AXIS_ROPE_DIM = HEAD_DIM // 2
REL_BUCKETS = 32
REL_MAX_DIST = 1024
D_FF = ((8 * D_MODEL + 3 * 256 - 1) // (3 * 256)) * 256
EPS = 1e-6
NEG_INF = -1e30

A_Q_W = N_A_HEADS * HEAD_DIM
A_KV_W = N_A_KV_HEADS * HEAD_DIM
B_W = N_B_HEADS * HEAD_DIM
B_OUT_W = N_B_HEADS_PER_GROUP * HEAD_DIM
IN_W = A_Q_W + 2 * A_KV_W + 3 * B_W + N_BRANCHES * D_MODEL

kernel_name = "hybrid_gqa_axialrope_dilated_swa_griffin_merge"


def rms_norm(x, g):
    x32 = x.astype(jnp.float32)
    y = x32 * lax.rsqrt(jnp.mean(x32 * x32, axis=-1, keepdims=True) + EPS)
    return y.astype(x.dtype) * g


def rotate(xh, ang):
    half = xh.shape[-1] // 2
    x1, x2 = xh[..., :half], xh[..., half:]
    c = jnp.cos(ang)[None, :, None, :].astype(xh.dtype)
    s = jnp.sin(ang)[None, :, None, :].astype(xh.dtype)
    return jnp.concatenate([x1 * c - x2 * s, x2 * c + x1 * s], axis=-1)


def axial_rope(x, ang_row, ang_col):
    return jnp.concatenate([rotate(x[..., :AXIS_ROPE_DIM], ang_row),
                            rotate(x[..., AXIS_ROPE_DIM:], ang_col)], axis=-1)


def axial_angles(seq):
    rows = seq // GRID_W
    row = jnp.broadcast_to(jnp.arange(rows)[:, None], (rows, GRID_W)).reshape(-1).astype(jnp.float32)
    col = jnp.broadcast_to(jnp.arange(GRID_W)[None, :], (rows, GRID_W)).reshape(-1).astype(jnp.float32)
    inv = ROPE_THETA ** (-jnp.arange(0, AXIS_ROPE_DIM, 2, dtype=jnp.float32) / AXIS_ROPE_DIM)
    return row[:, None] * inv, col[:, None] * inv


def t5_bucket(rel):
    nb = REL_BUCKETS // 2
    max_exact = nb // 2
    side = jnp.where(rel > 0, nb, 0)
    n = jnp.abs(rel)
    nf = jnp.maximum(n, 1).astype(jnp.float32)
    large = max_exact + (jnp.log(nf / max_exact) / math.log(REL_MAX_DIST / max_exact)
                         * (nb - max_exact)).astype(jnp.int32)
    large = jnp.minimum(large, nb - 1)
    return side + jnp.where(n < max_exact, n, large)


def grid_attention(q, k, v):
    b, s = q.shape[:2]
    nb = s // Q_BLOCK
    qg = q.reshape(b, nb, Q_BLOCK, N_A_KV_HEADS, N_A_GROUP, HEAD_DIM).swapaxes(0, 1)
    k32 = k.astype(jnp.float32)
    scale = HEAD_DIM ** -0.5

    def one_block(qb):
        logits = jnp.einsum('bqkgd,bskd->bkgqs', qb.astype(jnp.float32), k32) * scale
        p = jax.nn.softmax(logits, axis=-1).astype(v.dtype)
        return jnp.einsum('bkgqs,bskd->bqkgd', p, v)

    o = lax.map(one_block, qg)
    return o.swapaxes(0, 1).reshape(b, s, A_Q_W)


def dilated_attention(q, k, v, rel_bias):
    b, s = q.shape[:2]
    nb = s // Q_BLOCK
    scale = HEAD_DIM ** -0.5
    offs, biases = [], []
    for g, (window, dil) in enumerate(B_PATTERNS):
        radius = window // (2 * dil)
        off = jnp.arange(-radius, radius + 1, dtype=jnp.int32) * dil
        offs.append(off)
        tab = rel_bias[t5_bucket(off)][:, g * N_B_HEADS_PER_GROUP:(g + 1) * N_B_HEADS_PER_GROUP]
        biases.append(tab.T.astype(jnp.float32))
    k_groups = [k[:, :, g].astype(jnp.float32) for g in range(N_B_GROUPS)]
    v_groups = [v[:, :, g] for g in range(N_B_GROUPS)]

    def one_block(i):
        t0 = i * Q_BLOCK
        tq = t0 + jnp.arange(Q_BLOCK, dtype=jnp.int32)
        qb = lax.dynamic_slice_in_dim(q, t0, Q_BLOCK, axis=1).astype(jnp.float32)
        outs, lses = [], []
        for g in range(N_B_GROUPS):
            idx = tq[:, None] + offs[g][None, :]
            valid = (idx >= 0) & (idx < s)
            idxc = jnp.clip(idx, 0, s - 1)
            kg = jnp.take(k_groups[g], idxc, axis=1)
            vg = jnp.take(v_groups[g], idxc, axis=1)
            logits = jnp.einsum('bqhd,bqjhd->bhqj', qb[:, :, g], kg) * scale + biases[g][None, :, None, :]
            logits = jnp.where(valid[None, None], logits, NEG_INF)
            lse = jax.nn.logsumexp(logits, axis=-1)
            p = jnp.exp(logits - lse[..., None]).astype(v.dtype)
            outs.append(jnp.einsum('bhqj,bqjhd->bqhd', p, vg))
            lses.append(lse)
        w = jax.nn.softmax(jnp.stack(lses, axis=0), axis=0)
        w = jnp.swapaxes(w, 2, 3).astype(v.dtype)
        return jnp.einsum('gbqh,gbqhd->bqhd', w, jnp.stack(outs, axis=0))

    o = lax.map(one_block, jnp.arange(nb, dtype=jnp.int32))
    return o.swapaxes(0, 1).reshape(b, s, B_OUT_W)


def setup_inputs(seed: int = 0) -> dict:
    key = jax.random.key(seed)
    ks = jax.random.split(key, 16)
    f32 = jnp.float32

    def nrm(k, shape, scale):
        return jax.random.normal(k, shape, f32) * scale

    return {
        "x": nrm(ks[0], (BATCH, SEQ, D_MODEL), 1.0),
        "norm1_g": 1.0 + nrm(ks[1], (DEPTH, D_MODEL), 0.02),
        "w_in": nrm(ks[2], (DEPTH, D_MODEL, IN_W), D_MODEL ** -0.5),
        "b_gate": nrm(ks[3], (DEPTH, N_BRANCHES, D_MODEL), 0.02),
        "q_norm_a": 1.0 + nrm(ks[4], (DEPTH, HEAD_DIM), 0.02),
        "k_norm_a": 1.0 + nrm(ks[5], (DEPTH, HEAD_DIM), 0.02),
        "q_norm_b": 1.0 + nrm(ks[6], (DEPTH, HEAD_DIM), 0.02),
        "k_norm_b": 1.0 + nrm(ks[7], (DEPTH, HEAD_DIM), 0.02),
        "rel_bias": nrm(ks[8], (REL_BUCKETS, N_B_HEADS), 0.2),
        "w_proj_a": nrm(ks[9], (DEPTH, A_Q_W, D_MODEL), A_Q_W ** -0.5),
        "w_proj_b": nrm(ks[10], (DEPTH, B_OUT_W, D_MODEL), B_OUT_W ** -0.5),
        "w_out": nrm(ks[11], (DEPTH, D_MODEL, D_MODEL), D_MODEL ** -0.5),
        "norm2_g": 1.0 + nrm(ks[12], (DEPTH, D_MODEL), 0.02),
        "w_ffn_gate": nrm(ks[13], (DEPTH, D_MODEL, D_FF), D_MODEL ** -0.5),
        "w_ffn_up": nrm(ks[14], (DEPTH, D_MODEL, D_FF), D_MODEL ** -0.5),
        "w_ffn_down": nrm(ks[15], (DEPTH, D_FF, D_MODEL), D_FF ** -0.5),
    }


def reference(x, norm1_g, w_in, b_gate, q_norm_a, k_norm_a, q_norm_b, k_norm_b, rel_bias,
              w_proj_a, w_proj_b, w_out, norm2_g, w_ffn_gate, w_ffn_up, w_ffn_down):
    b, s, _ = x.shape
    ang_row, ang_col = axial_angles(s)
    splits = [int(c) for c in np.cumsum([A_Q_W, A_KV_W, A_KV_W, B_W, B_W, B_W, D_MODEL])]
    b_shape = (b, s, N_B_GROUPS, N_B_HEADS_PER_GROUP, HEAD_DIM)
    for l in range(DEPTH):
        h = rms_norm(x, norm1_g[l])
        proj = h @ w_in[l]
        qa, ka, va, qb, kb, vb, ga, gb = jnp.split(proj, splits, axis=-1)
        qa = axial_rope(rms_norm(qa.reshape(b, s, N_A_HEADS, HEAD_DIM), q_norm_a[l]), ang_row, ang_col)
        ka = axial_rope(rms_norm(ka.reshape(b, s, N_A_KV_HEADS, HEAD_DIM), k_norm_a[l]), ang_row, ang_col)
        va = va.reshape(b, s, N_A_KV_HEADS, HEAD_DIM)
        o_a = grid_attention(qa, ka, va)
        qb = rms_norm(qb.reshape(b_shape), q_norm_b[l])
        kb = rms_norm(kb.reshape(b_shape), k_norm_b[l])
        vb = vb.reshape(b_shape)
        o_b = dilated_attention(qb, kb, vb, rel_bias)
        gate_a = jax.nn.sigmoid(ga + b_gate[l, 0])
        gate_b = jax.nn.sigmoid(gb + b_gate[l, 1])
        merged = gate_a * (o_a @ w_proj_a[l]) + gate_b * (o_b @ w_proj_b[l])
        x = x + merged @ w_out[l]
        h = rms_norm(x, norm2_g[l])
        x = x + (jax.nn.silu(h @ w_ffn_gate[l]) * (h @ w_ffn_up[l])) @ w_ffn_down[l]
    return x
```

```python
import functools
import math

import jax
import jax.numpy as jnp
import numpy as np
from jax import lax
from jax.experimental import pallas as pl
from jax.experimental.pallas import tpu as pltpu

F32 = jnp.float32
BF16 = jnp.bfloat16

HEAD_DIM = 128
N_A_HEADS = 16
N_A_KV_HEADS = 4
N_A_GROUP = N_A_HEADS // N_A_KV_HEADS
B_PATTERNS = ((128, 1), (512, 4), (2048, 16))
N_B_GROUPS = len(B_PATTERNS)
N_B_HEADS_PER_GROUP = 4
GRID_W = 64
ROPE_THETA = 10000.0
AXIS_ROPE_DIM = HEAD_DIM // 2
REL_BUCKETS = 32
REL_MAX_DIST = 1024
EPS = 1e-6
NEG_INF = -1e30
ATTN_SCALE = HEAD_DIM ** -0.5

Q_TILE = 128
HEAD_BLOCK = N_A_GROUP * HEAD_DIM
VMEM_LIMIT = 56 * 1024 * 1024


def _cparams(sem):
    return pltpu.CompilerParams(dimension_semantics=sem, vmem_limit_bytes=VMEM_LIMIT)


def _rmsnorm_kernel(x_ref, g_ref, o_ref):
    x = x_ref[...]
    ms = jnp.mean(x * x, axis=-1, keepdims=True)
    o_ref[...] = (x * lax.rsqrt(ms + EPS) * g_ref[...]).astype(o_ref.dtype)


def _rmsnorm(x, g, *, tm=256, name):
    m, d = x.shape
    return pl.pallas_call(
        _rmsnorm_kernel,
        grid=(m // tm,),
        in_specs=[pl.BlockSpec((tm, d), lambda i: (i, 0)),
                  pl.BlockSpec((1, d), lambda i: (0, 0))],
        out_specs=pl.BlockSpec((tm, d), lambda i: (i, 0)),
        out_shape=jax.ShapeDtypeStruct((m, d), BF16),
        compiler_params=_cparams(("parallel",)),
        name=name,
    )(x, g.reshape(1, d))


def _mm_kernel(a_ref, w_ref, o_ref, wb_ref):
    @pl.when(pl.program_id(1) == 0)
    def _():
        wb_ref[...] = w_ref[...].astype(BF16)

    o_ref[...] = jnp.dot(a_ref[...], wb_ref[...], preferred_element_type=F32).astype(o_ref.dtype)


def _matmul(a, w, *, tm, tn, out_dtype, name):
    m, k = a.shape
    n = w.shape[1]
    return pl.pallas_call(
        _mm_kernel,
        grid=(n // tn, m // tm),
        in_specs=[pl.BlockSpec((tm, k), lambda j, i: (i, 0)),
                  pl.BlockSpec((k, tn), lambda j, i: (0, j))],
        out_specs=pl.BlockSpec((tm, tn), lambda j, i: (i, j)),
        out_shape=jax.ShapeDtypeStruct((m, n), out_dtype),
        scratch_shapes=[pltpu.VMEM((k, tn), BF16)],
        compiler_params=_cparams(("arbitrary", "arbitrary")),
        name=name,
    )(a, w)


def _mm_resid_kernel(a_ref, w_ref, r_ref, o_ref, wb_ref):
    @pl.when(pl.program_id(1) == 0)
    def _():
        wb_ref[...] = w_ref[...].astype(BF16)

    o_ref[...] = r_ref[...] + jnp.dot(a_ref[...], wb_ref[...], preferred_element_type=F32)


def _matmul_resid(a, w, resid, *, tm, tn, tk, kblock, name):
    m = a.shape[0]
    n = w.shape[1]
    return pl.pallas_call(
        _mm_resid_kernel,
        grid=(n // tn, m // tm),
        in_specs=[pl.BlockSpec((tm, tk), lambda j, i: (i, kblock)),
                  pl.BlockSpec((tk, tn), lambda j, i: (kblock, j)),
                  pl.BlockSpec((tm, tn), lambda j, i: (i, j))],
        out_specs=pl.BlockSpec((tm, tn), lambda j, i: (i, j)),
        out_shape=jax.ShapeDtypeStruct((m, n), F32),
        scratch_shapes=[pltpu.VMEM((tk, tn), BF16)],
        compiler_params=_cparams(("arbitrary", "arbitrary")),
        name=name,
    )(a, w, resid)


def _merge_kernel(oa_ref, ob_ref, ga_ref, gb_ref, bg_ref, wa_ref, wb_ref, o_ref, was_ref, wbs_ref):
    @pl.when(pl.program_id(1) == 0)
    def _():
        was_ref[...] = wa_ref[...].astype(BF16)
        wbs_ref[...] = wb_ref[...].astype(BF16)

    pa = jnp.dot(oa_ref[...], was_ref[...], preferred_element_type=F32)
    pb = jnp.dot(ob_ref[...], wbs_ref[...], preferred_element_type=F32)
    gate_a = jax.nn.sigmoid(ga_ref[...].astype(F32) + bg_ref[0:1, :])
    gate_b = jax.nn.sigmoid(gb_ref[...].astype(F32) + bg_ref[1:2, :])
    o_ref[...] = (gate_a * pa + gate_b * pb).astype(o_ref.dtype)


def _merge(o_a, o_b, proj, b_gate, w_proj_a, w_proj_b, *, ga_col, gb_col, tm, tn, name):
    m, ka = o_a.shape
    kb = o_b.shape[1]
    n = w_proj_a.shape[1]
    ga_blk, gb_blk = ga_col // tn, gb_col // tn
    return pl.pallas_call(
        _merge_kernel,
        grid=(n // tn, m // tm),
        in_specs=[pl.BlockSpec((tm, ka), lambda j, i: (i, 0)),
                  pl.BlockSpec((tm, kb), lambda j, i: (i, 0)),
                  pl.BlockSpec((tm, tn), lambda j, i: (i, ga_blk + j)),
                  pl.BlockSpec((tm, tn), lambda j, i: (i, gb_blk + j)),
                  pl.BlockSpec((2, tn), lambda j, i: (0, j)),
                  pl.BlockSpec((ka, tn), lambda j, i: (0, j)),
                  pl.BlockSpec((kb, tn), lambda j, i: (0, j))],
        out_specs=pl.BlockSpec((tm, tn), lambda j, i: (i, j)),
        out_shape=jax.ShapeDtypeStruct((m, n), BF16),
        scratch_shapes=[pltpu.VMEM((ka, tn), BF16), pltpu.VMEM((kb, tn), BF16)],
        compiler_params=_cparams(("arbitrary", "arbitrary")),
        name=name,
    )(o_a, o_b, proj, proj, b_gate, w_proj_a, w_proj_b)


def _swiglu_kernel(h_ref, wg_ref, wu_ref, o_ref, wgs_ref, wus_ref):
    @pl.when(pl.program_id(1) == 0)
    def _():
        wgs_ref[...] = wg_ref[...].astype(BF16)
        wus_ref[...] = wu_ref[...].astype(BF16)

    h = h_ref[...]
    g = jnp.dot(h, wgs_ref[...], preferred_element_type=F32)
    u = jnp.dot(h, wus_ref[...], preferred_element_type=F32)
    o_ref[...] = (g * jax.nn.sigmoid(g) * u).astype(o_ref.dtype)


def _swiglu(h, w_gate, w_up, *, tm, tn, name):
    m, k = h.shape
    n = w_gate.shape[1]
    return pl.pallas_call(
        _swiglu_kernel,
        grid=(n // tn, m // tm),
        in_specs=[pl.BlockSpec((tm, k), lambda j, i: (i, 0)),
                  pl.BlockSpec((k, tn), lambda j, i: (0, j)),
                  pl.BlockSpec((k, tn), lambda j, i: (0, j))],
        out_specs=pl.BlockSpec((tm, tn), lambda j, i: (i, j)),
        out_shape=jax.ShapeDtypeStruct((m, n), BF16),
        scratch_shapes=[pltpu.VMEM((k, tn), BF16), pltpu.VMEM((k, tn), BF16)],
        compiler_params=_cparams(("arbitrary", "arbitrary")),
        name=name,
    )(h, w_gate, w_up)


def _head_rms(x, g):
    ms = jnp.mean(x * x, axis=-1, keepdims=True)
    return x * lax.rsqrt(ms + EPS) * g


def _rope(x, cos, sin_signed):
    lane = lax.broadcasted_iota(jnp.int32, x.shape, 1)
    first_half = (lane % AXIS_ROPE_DIM) < (AXIS_ROPE_DIM // 2)
    partner = jnp.where(first_half,
                        pltpu.roll(x, HEAD_DIM - AXIS_ROPE_DIM // 2, 1),
                        pltpu.roll(x, AXIS_ROPE_DIM // 2, 1))
    return x * cos + partner * sin_signed


def _rope_tables(seq):
    rows = seq // GRID_W
    row = jnp.broadcast_to(jnp.arange(rows)[:, None], (rows, GRID_W)).reshape(-1).astype(F32)
    col = jnp.broadcast_to(jnp.arange(GRID_W)[None, :], (rows, GRID_W)).reshape(-1).astype(F32)
    inv = ROPE_THETA ** (-jnp.arange(0, AXIS_ROPE_DIM, 2, dtype=F32) / AXIS_ROPE_DIM)
    ang_row, ang_col = row[:, None] * inv, col[:, None] * inv
    cr, sr, cc, sc = jnp.cos(ang_row), jnp.sin(ang_row), jnp.cos(ang_col), jnp.sin(ang_col)
    cos = jnp.concatenate([cr, cr, cc, cc], axis=-1)
    sin_signed = jnp.concatenate([-sr, sr, -sc, sc], axis=-1)
    return cos, sin_signed


def _attn_a_kernel(q_ref, k_ref, v_ref, cq_ref, sq_ref, ck_ref, sk_ref, gq_ref, gk_ref,
                   o_ref, kn_ref):
    @pl.when(pl.program_id(2) == 0)
    def _():
        k = _head_rms(k_ref[0].astype(F32), gk_ref[...])
        kn_ref[...] = _rope(k, ck_ref[...], sk_ref[...]).astype(BF16)

    kn = kn_ref[...]
    v = v_ref[0]
    cq, sq, gq = cq_ref[...], sq_ref[...], gq_ref[...]
    for g in range(N_A_GROUP):
        cols = slice(g * HEAD_DIM, (g + 1) * HEAD_DIM)
        q = _head_rms(q_ref[0, :, cols].astype(F32), gq)
        q = (_rope(q, cq, sq) * ATTN_SCALE).astype(BF16)
        logits = lax.dot_general(q, kn, (((1,), (1,)), ((), ())), preferred_element_type=F32)
        m = jnp.max(logits, axis=-1, keepdims=True)
        p = jnp.exp(logits - m)
        denom = jnp.sum(p, axis=-1, keepdims=True)
        o = jnp.dot(p.astype(BF16), v, preferred_element_type=F32) / denom
        o_ref[0, :, cols] = o.astype(o_ref.dtype)


def _attention_a(proj3, cos, sin_signed, gq, gk, *, q_col, k_col, v_col, tq):
    b, s, _ = proj3.shape
    qb, kb, vb = q_col // HEAD_BLOCK, k_col // HEAD_DIM, v_col // HEAD_DIM
    return pl.pallas_call(
        _attn_a_kernel,
        grid=(b, N_A_KV_HEADS, s // tq),
        in_specs=[pl.BlockSpec((1, tq, HEAD_BLOCK), lambda bi, kv, qi: (bi, qi, qb + kv)),
                  pl.BlockSpec((1, s, HEAD_DIM), lambda bi, kv, qi: (bi, 0, kb + kv)),
                  pl.BlockSpec((1, s, HEAD_DIM), lambda bi, kv, qi: (bi, 0, vb + kv)),
                  pl.BlockSpec((tq, HEAD_DIM), lambda bi, kv, qi: (qi, 0)),
                  pl.BlockSpec((tq, HEAD_DIM), lambda bi, kv, qi: (qi, 0)),
                  pl.BlockSpec((s, HEAD_DIM), lambda bi, kv, qi: (0, 0)),
                  pl.BlockSpec((s, HEAD_DIM), lambda bi, kv, qi: (0, 0)),
                  pl.BlockSpec((1, HEAD_DIM), lambda bi, kv, qi: (0, 0)),
                  pl.BlockSpec((1, HEAD_DIM), lambda bi, kv, qi: (0, 0))],
        out_specs=pl.BlockSpec((1, tq, HEAD_BLOCK), lambda bi, kv, qi: (bi, qi, kv)),
        out_shape=jax.ShapeDtypeStruct((b, s, N_A_HEADS * HEAD_DIM), BF16),
        scratch_shapes=[pltpu.VMEM((s, HEAD_DIM), BF16)],
        compiler_params=_cparams(("arbitrary", "arbitrary", "arbitrary")),
        name="attention_a",
    )(proj3, proj3, proj3, cos, sin_signed, cos, sin_signed, gq, gk)


def _t5_bucket(rel):
    nb = REL_BUCKETS // 2
    max_exact = nb // 2
    side = jnp.where(rel > 0, nb, 0)
    n = jnp.abs(rel)
    nf = jnp.maximum(n, 1).astype(F32)
    large = max_exact + (jnp.log(nf / max_exact) / math.log(REL_MAX_DIST / max_exact)
                         * (nb - max_exact)).astype(jnp.int32)
    large = jnp.minimum(large, nb - 1)
    return side + jnp.where(n < max_exact, n, large)


def _band_geometry(seq, dil):
    length = seq // dil
    kw = min(2 * Q_TILE, length)
    nblk = length // Q_TILE
    starts = [min(max(mb * Q_TILE - Q_TILE // 2, 0), length - kw) for mb in range(nblk)]
    shifts = sorted({st - mb * Q_TILE for mb, st in enumerate(starts)}, reverse=True)
    return length, kw, nblk, shifts


def _bias_tiles(rel_bias, group, seq, radius, dil):
    _, kw, _, shifts = _band_geometry(seq, dil)
    i = np.arange(Q_TILE)[:, None]
    j = np.arange(kw)[None, :]
    tiles = []
    for shift in shifts:
        rel = j - i + shift
        valid = jnp.asarray(np.abs(rel) <= radius)
        bucket = _t5_bucket(jnp.asarray(rel * dil, dtype=jnp.int32))
        heads = slice(group * N_B_HEADS_PER_GROUP, (group + 1) * N_B_HEADS_PER_GROUP)
        bias = jnp.moveaxis(rel_bias[bucket][:, :, heads], -1, 0).astype(F32)
        tiles.append(jnp.where(valid[None], bias, NEG_INF))
    return jnp.stack(tiles, axis=0)


def _attn_b_kernel(q_ref, k_ref, v_ref, tile_ref, gq_ref, gk_ref, o_ref, lse_ref, *, length, kw):
    if length == kw:
        start = 0
    else:
        m0 = pl.program_id(2) * Q_TILE
        start = pl.multiple_of(jnp.clip(m0 - Q_TILE // 2, 0, length - kw), Q_TILE // 2)
    gq, gk = gq_ref[...], gk_ref[...]
    for h in range(N_B_HEADS_PER_GROUP):
        cols = slice(h * HEAD_DIM, (h + 1) * HEAD_DIM)
        q = (_head_rms(q_ref[0, :, cols].astype(F32), gq) * ATTN_SCALE).astype(BF16)
        k = _head_rms(k_ref[0, pl.ds(start, kw), cols].astype(F32), gk).astype(BF16)
        v = v_ref[0, pl.ds(start, kw), cols]
        logits = lax.dot_general(q, k, (((1,), (1,)), ((), ())), preferred_element_type=F32)
        logits = logits + tile_ref[0, h]
        m = jnp.max(logits, axis=-1, keepdims=True)
        p = jnp.exp(logits - m)
        denom = jnp.sum(p, axis=-1, keepdims=True)
        o_ref[0, :, cols] = jnp.dot(p.astype(BF16), v, preferred_element_type=F32) / denom
        lse_ref[0, :, cols] = jnp.broadcast_to(m + jnp.log(denom), (Q_TILE, HEAD_DIM))


def _attention_b_group(proj3, tiles, gq, gk, *, group, dil, q_col, k_col, v_col):
    b, s, in_w = proj3.shape
    length, kw, nblk, shifts = _band_geometry(s, dil)
    cols_per_class = in_w // HEAD_BLOCK
    qb = q_col // HEAD_BLOCK + group
    kb = k_col // HEAD_BLOCK + group
    vb = v_col // HEAD_BLOCK + group
    proj_cls = proj3.reshape(b, length, dil * in_w)
    starts = [min(max(mb * Q_TILE - Q_TILE // 2, 0), length - kw) for mb in range(nblk)]
    variant_of_block = np.array([shifts.index(st - mb * Q_TILE) for mb, st in enumerate(starts)])
    assert all(variant_of_block[mb] == (0 if mb == 0 else len(shifts) - 1 if mb == nblk - 1 else 1)
               for mb in range(nblk)) or nblk == 1
    nvar = len(shifts)

    def variant(mb):
        if nvar == 1:
            return 0
        return jnp.where(mb == 0, 0, jnp.where(mb == nblk - 1, nvar - 1, 1))

    out_shape = jax.ShapeDtypeStruct((b, length, dil * HEAD_BLOCK), F32)
    out_spec = pl.BlockSpec((1, Q_TILE, HEAD_BLOCK), lambda bi, c, mb: (bi, mb, c))
    o, lse = pl.pallas_call(
        functools.partial(_attn_b_kernel, length=length, kw=kw),
        grid=(b, dil, nblk),
        in_specs=[pl.BlockSpec((1, Q_TILE, HEAD_BLOCK),
                               lambda bi, c, mb: (bi, mb, c * cols_per_class + qb)),
                  pl.BlockSpec((1, length, HEAD_BLOCK),
                               lambda bi, c, mb: (bi, 0, c * cols_per_class + kb)),
                  pl.BlockSpec((1, length, HEAD_BLOCK),
                               lambda bi, c, mb: (bi, 0, c * cols_per_class + vb)),
                  pl.BlockSpec((1, N_B_HEADS_PER_GROUP, Q_TILE, kw),
                               lambda bi, c, mb: (variant(mb), 0, 0, 0)),
                  pl.BlockSpec((1, HEAD_DIM), lambda bi, c, mb: (0, 0)),
                  pl.BlockSpec((1, HEAD_DIM), lambda bi, c, mb: (0, 0))],
        out_specs=[out_spec, out_spec],
        out_shape=[out_shape, out_shape],
        compiler_params=_cparams(("parallel", "parallel", "parallel")),
        name=f"attention_b_dil{dil}",
    )(proj_cls, proj_cls, proj_cls, tiles, gq, gk)
    return o.reshape(b * s, HEAD_BLOCK), lse.reshape(b * s, HEAD_BLOCK)


def _combine_kernel(o0_ref, o1_ref, o2_ref, l0_ref, l1_ref, l2_ref, o_ref):
    l0, l1, l2 = l0_ref[...], l1_ref[...], l2_ref[...]
    mx = jnp.maximum(jnp.maximum(l0, l1), l2)
    e0, e1, e2 = jnp.exp(l0 - mx), jnp.exp(l1 - mx), jnp.exp(l2 - mx)
    num = e0 * o0_ref[...] + e1 * o1_ref[...] + e2 * o2_ref[...]
    o_ref[...] = (num / (e0 + e1 + e2)).astype(o_ref.dtype)


def _combine(outs, lses, *, tm=512):
    m, n = outs[0].shape
    spec = pl.BlockSpec((tm, n), lambda i: (i, 0))
    return pl.pallas_call(
        _combine_kernel,
        grid=(m // tm,),
        in_specs=[spec] * 6,
        out_specs=spec,
        out_shape=jax.ShapeDtypeStruct((m, n), BF16),
        compiler_params=_cparams(("parallel",)),
        name="attention_b_combine",
    )(*outs, *lses)


def kernel(x, norm1_g, w_in, b_gate, q_norm_a, k_norm_a, q_norm_b, k_norm_b, rel_bias,
           w_proj_a, w_proj_b, w_out, norm2_g, w_ffn_gate, w_ffn_up, w_ffn_down):
    b, s, d = x.shape
    m = b * s
    a_q_w = N_A_HEADS * HEAD_DIM
    a_kv_w = N_A_KV_HEADS * HEAD_DIM
    b_w = N_B_GROUPS * N_B_HEADS_PER_GROUP * HEAD_DIM
    col = np.cumsum([0, a_q_w, a_kv_w, a_kv_w, b_w, b_w, b_w, d])
    qa_c, ka_c, va_c, qb_c, kb_c, vb_c, ga_c, gb_c = (int(c) for c in col)
    cos, sin_signed = _rope_tables(s)
    d_ff = w_ffn_gate.shape[-1]
    tk_down = d_ff // 2

    xf = x.reshape(m, d)
    for l in range(norm1_g.shape[0]):
        h = _rmsnorm(xf, norm1_g[l], name="rmsnorm1")
        proj = _matmul(h, w_in[l], tm=1024, tn=512, out_dtype=BF16, name="in_proj")
        proj3 = proj.reshape(b, s, proj.shape[-1])

        o_a = _attention_a(proj3, cos, sin_signed, q_norm_a[l].reshape(1, -1),
                           k_norm_a[l].reshape(1, -1), q_col=qa_c, k_col=ka_c, v_col=va_c, tq=256)

        outs, lses = [], []
        for g, (window, dil) in enumerate(B_PATTERNS):
            tiles = _bias_tiles(rel_bias, g, s, window // (2 * dil), dil)
            o_g, lse_g = _attention_b_group(proj3, tiles, q_norm_b[l].reshape(1, -1),
                                            k_norm_b[l].reshape(1, -1), group=g, dil=dil,
                                            q_col=qb_c, k_col=kb_c, v_col=vb_c)
            outs.append(o_g)
            lses.append(lse_g)
        o_b = _combine(outs, lses)

        merged = _merge(o_a.reshape(m, a_q_w), o_b, proj, b_gate[l], w_proj_a[l], w_proj_b[l],
                        ga_col=ga_c, gb_col=gb_c, tm=1024, tn=512, name="gated_merge")
        xf = _matmul_resid(merged, w_out[l], xf, tm=1024, tn=512, tk=d, kblock=0, name="out_proj")

        h2 = _rmsnorm(xf, norm2_g[l], name="rmsnorm2")
        act = _swiglu(h2, w_ffn_gate[l], w_ffn_up[l], tm=1024, tn=256, name="swiglu_up")
        for kblock in range(d_ff // tk_down):
            xf = _matmul_resid(act, w_ffn_down[l], xf, tm=512, tn=512, tk=tk_down, kblock=kblock,
                               name=f"ffn_down{kblock}")
    return xf.reshape(b, s, d)
```

```python
import functools
import math

import jax
import jax.numpy as jnp
import numpy as np
from jax import lax
from jax.experimental import pallas as pl
from jax.experimental.pallas import tpu as pltpu

F32 = jnp.float32
BF16 = jnp.bfloat16

HEAD_DIM = 128
N_A_HEADS = 16
N_A_KV_HEADS = 4
N_A_GROUP = N_A_HEADS // N_A_KV_HEADS
B_PATTERNS = ((128, 1), (512, 4), (2048, 16))
N_B_GROUPS = len(B_PATTERNS)
N_B_HEADS_PER_GROUP = 4
GRID_W = 64
ROPE_THETA = 10000.0
AXIS_ROPE_DIM = HEAD_DIM // 2
REL_BUCKETS = 32
REL_MAX_DIST = 1024
EPS = 1e-6
NEG_INF = -1e30
ATTN_SCALE = HEAD_DIM ** -0.5

Q_TILE = 128
HEAD_BLOCK = N_A_GROUP * HEAD_DIM
VMEM_LIMIT = 56 * 1024 * 1024


def _cparams(sem):
    return pltpu.CompilerParams(dimension_semantics=sem, vmem_limit_bytes=VMEM_LIMIT)


def _rmsnorm_kernel(x_ref, g_ref, o_ref):
    x = x_ref[...]
    ms = jnp.mean(x * x, axis=-1, keepdims=True)
    o_ref[...] = (x * lax.rsqrt(ms + EPS) * g_ref[...]).astype(o_ref.dtype)


def _rmsnorm(x, g, *, tm=256, name):
    m, d = x.shape
    return pl.pallas_call(
        _rmsnorm_kernel,
        grid=(m // tm,),
        in_specs=[pl.BlockSpec((tm, d), lambda i: (i, 0)),
                  pl.BlockSpec((1, d), lambda i: (0, 0))],
        out_specs=pl.BlockSpec((tm, d), lambda i: (i, 0)),
        out_shape=jax.ShapeDtypeStruct((m, d), BF16),
        compiler_params=_cparams(("parallel",)),
        name=name,
    )(x, g.reshape(1, d))


def _mm_kernel(a_ref, w_ref, o_ref, wb_ref):
    @pl.when(pl.program_id(1) == 0)
    def _():
        wb_ref[...] = w_ref[...].astype(BF16)

    o_ref[...] = jnp.dot(a_ref[...], wb_ref[...], preferred_element_type=F32).astype(o_ref.dtype)


def _matmul(a, w, *, col0, ncols, tm, tn, out_dtype, name):
    m, k = a.shape
    jb = col0 // tn
    return pl.pallas_call(
        _mm_kernel,
        grid=(ncols // tn, m // tm),
        in_specs=[pl.BlockSpec((tm, k), lambda j, i: (i, 0)),
                  pl.BlockSpec((k, tn), lambda j, i: (0, jb + j))],
        out_specs=pl.BlockSpec((tm, tn), lambda j, i: (i, j)),
        out_shape=jax.ShapeDtypeStruct((m, ncols), out_dtype),
        scratch_shapes=[pltpu.VMEM((k, tn), BF16)],
        compiler_params=_cparams(("arbitrary", "arbitrary")),
        name=name,
    )(a, w)


def _mm_resid_kernel(a_ref, w_ref, r_ref, o_ref, wb_ref):
    @pl.when(pl.program_id(1) == 0)
    def _():
        wb_ref[...] = w_ref[...].astype(BF16)

    o_ref[...] = r_ref[...] + jnp.dot(a_ref[...], wb_ref[...], preferred_element_type=F32)


def _matmul_resid(a, w, resid, *, tm, tn, tk, kblock, name):
    m = a.shape[0]
    n = w.shape[1]
    return pl.pallas_call(
        _mm_resid_kernel,
        grid=(n // tn, m // tm),
        in_specs=[pl.BlockSpec((tm, tk), lambda j, i: (i, kblock)),
                  pl.BlockSpec((tk, tn), lambda j, i: (kblock, j)),
                  pl.BlockSpec((tm, tn), lambda j, i: (i, j))],
        out_specs=pl.BlockSpec((tm, tn), lambda j, i: (i, j)),
        out_shape=jax.ShapeDtypeStruct((m, n), F32),
        scratch_shapes=[pltpu.VMEM((tk, tn), BF16)],
        compiler_params=_cparams(("arbitrary", "arbitrary")),
        name=name,
    )(a, w, resid)


def _merge_kernel(oa_ref, ob_ref, ga_ref, gb_ref, bg_ref, wa_ref, wb_ref, o_ref, was_ref, wbs_ref):
    @pl.when(pl.program_id(1) == 0)
    def _():
        was_ref[...] = wa_ref[...].astype(BF16)
        wbs_ref[...] = wb_ref[...].astype(BF16)

    pa = jnp.dot(oa_ref[...], was_ref[...], preferred_element_type=F32)
    pb = jnp.dot(ob_ref[...], wbs_ref[...], preferred_element_type=F32)
    gate_a = jax.nn.sigmoid(ga_ref[...].astype(F32) + bg_ref[0:1, :])
    gate_b = jax.nn.sigmoid(gb_ref[...].astype(F32) + bg_ref[1:2, :])
    o_ref[...] = (gate_a * pa + gate_b * pb).astype(o_ref.dtype)


def _merge(o_a, o_b, gates, b_gate, w_proj_a, w_proj_b, *, tm, tn, name):
    m, ka = o_a.shape
    kb = o_b.shape[1]
    n = w_proj_a.shape[1]
    gb_blk = n // tn
    return pl.pallas_call(
        _merge_kernel,
        grid=(n // tn, m // tm),
        in_specs=[pl.BlockSpec((tm, ka), lambda j, i: (i, 0)),
                  pl.BlockSpec((tm, kb), lambda j, i: (i, 0)),
                  pl.BlockSpec((tm, tn), lambda j, i: (i, j)),
                  pl.BlockSpec((tm, tn), lambda j, i: (i, gb_blk + j)),
                  pl.BlockSpec((2, tn), lambda j, i: (0, j)),
                  pl.BlockSpec((ka, tn), lambda j, i: (0, j)),
                  pl.BlockSpec((kb, tn), lambda j, i: (0, j))],
        out_specs=pl.BlockSpec((tm, tn), lambda j, i: (i, j)),
        out_shape=jax.ShapeDtypeStruct((m, n), BF16),
        scratch_shapes=[pltpu.VMEM((ka, tn), BF16), pltpu.VMEM((kb, tn), BF16)],
        compiler_params=_cparams(("arbitrary", "arbitrary")),
        name=name,
    )(o_a, o_b, gates, gates, b_gate, w_proj_a, w_proj_b)


def _swiglu_kernel(h_ref, wg_ref, wu_ref, o_ref, wgs_ref, wus_ref):
    @pl.when(pl.program_id(1) == 0)
    def _():
        wgs_ref[...] = wg_ref[...].astype(BF16)
        wus_ref[...] = wu_ref[...].astype(BF16)

    h = h_ref[...]
    g = jnp.dot(h, wgs_ref[...], preferred_element_type=F32)
    u = jnp.dot(h, wus_ref[...], preferred_element_type=F32)
    o_ref[...] = (g * jax.nn.sigmoid(g) * u).astype(o_ref.dtype)


def _swiglu(h, w_gate, w_up, *, tm, tn, name):
    m, k = h.shape
    n = w_gate.shape[1]
    return pl.pallas_call(
        _swiglu_kernel,
        grid=(n // tn, m // tm),
        in_specs=[pl.BlockSpec((tm, k), lambda j, i: (i, 0)),
                  pl.BlockSpec((k, tn), lambda j, i: (0, j)),
                  pl.BlockSpec((k, tn), lambda j, i: (0, j))],
        out_specs=pl.BlockSpec((tm, tn), lambda j, i: (i, j)),
        out_shape=jax.ShapeDtypeStruct((m, n), BF16),
        scratch_shapes=[pltpu.VMEM((k, tn), BF16), pltpu.VMEM((k, tn), BF16)],
        compiler_params=_cparams(("arbitrary", "arbitrary")),
        name=name,
    )(h, w_gate, w_up)


def _head_rms(x, g):
    ms = jnp.mean(x * x, axis=-1, keepdims=True)
    return x * lax.rsqrt(ms + EPS) * g


def _rope(x, cos, sin_signed):
    lane = lax.broadcasted_iota(jnp.int32, x.shape, 1)
    first_half = (lane % AXIS_ROPE_DIM) < (AXIS_ROPE_DIM // 2)
    partner = jnp.where(first_half,
                        pltpu.roll(x, HEAD_DIM - AXIS_ROPE_DIM // 2, 1),
                        pltpu.roll(x, AXIS_ROPE_DIM // 2, 1))
    return x * cos + partner * sin_signed


def _rope_tables(seq):
    rows = seq // GRID_W
    row = jnp.broadcast_to(jnp.arange(rows)[:, None], (rows, GRID_W)).reshape(-1).astype(F32)
    col = jnp.broadcast_to(jnp.arange(GRID_W)[None, :], (rows, GRID_W)).reshape(-1).astype(F32)
    inv = ROPE_THETA ** (-jnp.arange(0, AXIS_ROPE_DIM, 2, dtype=F32) / AXIS_ROPE_DIM)
    ang_row, ang_col = row[:, None] * inv, col[:, None] * inv
    cr, sr, cc, sc = jnp.cos(ang_row), jnp.sin(ang_row), jnp.cos(ang_col), jnp.sin(ang_col)
    cos = jnp.concatenate([cr, cr, cc, cc], axis=-1)
    sin_signed = jnp.concatenate([-sr, sr, -sc, sc], axis=-1)
    return cos, sin_signed


def _attn_a_kernel(q_ref, k_ref, v_ref, cq_ref, sq_ref, ck_ref, sk_ref, gq_ref, gk_ref,
                   o_ref, kn_ref):
    @pl.when(pl.program_id(2) == 0)
    def _():
        k = _head_rms(k_ref[0].astype(F32), gk_ref[...])
        kn_ref[...] = _rope(k, ck_ref[...], sk_ref[...]).astype(BF16)

    kn = kn_ref[...]
    v = v_ref[0]
    cq, sq, gq = cq_ref[...], sq_ref[...], gq_ref[...]
    for g in range(N_A_GROUP):
        cols = slice(g * HEAD_DIM, (g + 1) * HEAD_DIM)
        q = _head_rms(q_ref[0, :, cols].astype(F32), gq)
        q = (_rope(q, cq, sq) * ATTN_SCALE).astype(BF16)
        logits = lax.dot_general(q, kn, (((1,), (1,)), ((), ())), preferred_element_type=F32)
        m = jnp.max(logits, axis=-1, keepdims=True)
        p = jnp.exp(logits - m)
        denom = jnp.sum(p, axis=-1, keepdims=True)
        o = jnp.dot(p.astype(BF16), v, preferred_element_type=F32) / denom
        o_ref[0, :, cols] = o.astype(o_ref.dtype)


def _attention_a(qkv3, cos, sin_signed, gq, gk, *, tq):
    b, s, _ = qkv3.shape
    kb = N_A_HEADS
    vb = N_A_HEADS + N_A_KV_HEADS
    return pl.pallas_call(
        _attn_a_kernel,
        grid=(b, N_A_KV_HEADS, s // tq),
        in_specs=[pl.BlockSpec((1, tq, HEAD_BLOCK), lambda bi, kv, qi: (bi, qi, kv)),
                  pl.BlockSpec((1, s, HEAD_DIM), lambda bi, kv, qi: (bi, 0, kb + kv)),
                  pl.BlockSpec((1, s, HEAD_DIM), lambda bi, kv, qi: (bi, 0, vb + kv)),
                  pl.BlockSpec((tq, HEAD_DIM), lambda bi, kv, qi: (qi, 0)),
                  pl.BlockSpec((tq, HEAD_DIM), lambda bi, kv, qi: (qi, 0)),
                  pl.BlockSpec((s, HEAD_DIM), lambda bi, kv, qi: (0, 0)),
                  pl.BlockSpec((s, HEAD_DIM), lambda bi, kv, qi: (0, 0)),
                  pl.BlockSpec((1, HEAD_DIM), lambda bi, kv, qi: (0, 0)),
                  pl.BlockSpec((1, HEAD_DIM), lambda bi, kv, qi: (0, 0))],
        out_specs=pl.BlockSpec((1, tq, HEAD_BLOCK), lambda bi, kv, qi: (bi, qi, kv)),
        out_shape=jax.ShapeDtypeStruct((b, s, N_A_HEADS * HEAD_DIM), BF16),
        scratch_shapes=[pltpu.VMEM((s, HEAD_DIM), BF16)],
        compiler_params=_cparams(("arbitrary", "arbitrary", "arbitrary")),
        name="attention_a",
    )(qkv3, qkv3, qkv3, cos, sin_signed, cos, sin_signed, gq, gk)


RADIUS = 64
assert all(w // (2 * d) == RADIUS for w, d in B_PATTERNS)


def _t5_bucket_np(rel):
    nb = REL_BUCKETS // 2
    max_exact = nb // 2
    side = np.where(rel > 0, nb, 0)
    n = np.abs(rel)
    nf = np.maximum(n, 1).astype(np.float32)
    large = max_exact + (np.log(nf / np.float32(max_exact))
                         / np.float32(math.log(REL_MAX_DIST / max_exact))
                         * np.float32(nb - max_exact)).astype(np.int32)
    large = np.minimum(large, nb - 1)
    return (side + np.where(n < max_exact, n, large)).astype(np.int32)


def _band_geometry(seq, dil):
    length = seq // dil
    kw = min(2 * Q_TILE, length)
    nblk = length // Q_TILE
    starts = [min(max(mb * Q_TILE - RADIUS, 0), length - kw) for mb in range(nblk)]
    shifts = sorted({st - mb * Q_TILE for mb, st in enumerate(starts)}, reverse=True)
    return length, kw, nblk, shifts


def _bucket_tiles(seq, dil):
    _, kw, _, shifts = _band_geometry(seq, dil)
    i = np.arange(Q_TILE)[:, None]
    j = np.arange(kw)[None, :]
    tiles = []
    for shift in shifts:
        rel = j - i + shift
        tiles.append(np.where(np.abs(rel) <= RADIUS, _t5_bucket_np(rel * dil), -1))
    return np.stack(tiles, axis=0).astype(np.int32)


def _attn_b_kernel(rb_ref, bk0_ref, bk1_ref, bk2_ref,
                   q0_ref, q1_ref, q2_ref, k0_ref, k1_ref, k2_ref, v0_ref, v1_ref, v2_ref,
                   gq_ref, gk_ref, o_ref,
                   qc_ref, kc_ref, vc_ref, oc_ref, lc_ref, on_ref, ln_ref,
                   t0_ref, t1_ref, t2_ref, *, seq):
    head = pl.program_id(1)
    gq, gk = gq_ref[...], gk_ref[...]
    bk_refs = (bk0_ref, bk1_ref, bk2_ref)
    tile_refs = (t0_ref, t1_ref, t2_ref)
    q_refs, k_refs, v_refs = (q0_ref, q1_ref, q2_ref), (k0_ref, k1_ref, k2_ref), (v0_ref, v1_ref, v2_ref)

    for g, (_, dil) in enumerate(B_PATTERNS):
        length, kw, nblk, shifts = _band_geometry(seq, dil)
        nvar = len(shifts)
        col = g * N_B_HEADS_PER_GROUP + head

        for var in range(nvar):
            bkt = bk_refs[g][var]

            def pick(b, acc, bkt=bkt, col=col):
                return jnp.where(bkt == b, rb_ref[b, col], acc)

            tile_refs[g][var] = lax.fori_loop(0, REL_BUCKETS, pick,
                                              jnp.full(bkt.shape, NEG_INF, F32))

        for c in range(dil):
            rows = pl.ds(c, length, stride=dil) if dil > 1 else pl.ds(0, length)
            dst = pl.ds(c * length, length)
            qc_ref[g, dst, :] = (_head_rms(q_refs[g][0, rows, :], gq) * ATTN_SCALE).astype(BF16)
            kc_ref[g, dst, :] = _head_rms(k_refs[g][0, rows, :], gk).astype(BF16)
            vc_ref[g, dst, :] = v_refs[g][0, rows, :].astype(BF16)

        def block(i, carry, g=g, length=length, kw=kw, nblk=nblk, nvar=nvar):
            r0 = pl.multiple_of(i * Q_TILE, Q_TILE)
            if nblk == 1:
                ks, var = r0, 0
            else:
                c, mb = i // nblk, i % nblk
                ks = c * length + jnp.clip(mb * Q_TILE - RADIUS, 0, length - kw)
                ks = pl.multiple_of(ks, RADIUS)
                var = jnp.where(mb == 0, 0, jnp.where(mb == nblk - 1, nvar - 1, 1))
            q = qc_ref[g, pl.ds(r0, Q_TILE), :]
            k = kc_ref[g, pl.ds(ks, kw), :]
            v = vc_ref[g, pl.ds(ks, kw), :]
            logits = lax.dot_general(q, k, (((1,), (1,)), ((), ())), preferred_element_type=F32)
            logits = logits + tile_refs[g][var]
            m = jnp.max(logits, axis=-1, keepdims=True)
            p = jnp.exp(logits - m)
            denom = jnp.sum(p, axis=-1, keepdims=True)
            o = jnp.dot(p.astype(BF16), v, preferred_element_type=F32) / denom
            oc_ref[g, pl.ds(r0, Q_TILE), :] = o
            lc_ref[g, pl.ds(r0, Q_TILE), :] = jnp.broadcast_to(m + jnp.log(denom), (Q_TILE, HEAD_DIM))
            return carry

        lax.fori_loop(0, seq // Q_TILE, block, 0)

        for c in range(dil):
            rows = pl.ds(c, length, stride=dil) if dil > 1 else pl.ds(0, length)
            src = pl.ds(c * length, length)
            on_ref[g, rows, :] = oc_ref[g, src, :]
            ln_ref[g, rows, :] = lc_ref[g, src, :]

    chunk = 2 * Q_TILE

    def combine(i, carry):
        rows = pl.ds(pl.multiple_of(i * chunk, chunk), chunk)
        l0, l1, l2 = ln_ref[0, rows, :], ln_ref[1, rows, :], ln_ref[2, rows, :]
        mx = jnp.maximum(jnp.maximum(l0, l1), l2)
        e0, e1, e2 = jnp.exp(l0 - mx), jnp.exp(l1 - mx), jnp.exp(l2 - mx)
        num = e0 * on_ref[0, rows, :] + e1 * on_ref[1, rows, :] + e2 * on_ref[2, rows, :]
        o_ref[0, rows, :] = (num / (e0 + e1 + e2)).astype(o_ref.dtype)
        return carry

    lax.fori_loop(0, seq // chunk, combine, 0)


def _attention_b(qkv3, rel_bias, gq, gk):
    b, s, _ = qkv3.shape
    nh = N_B_HEADS_PER_GROUP
    bks = [jnp.asarray(_bucket_tiles(s, dil)) for _, dil in B_PATTERNS]

    def head_spec(part, g):
        base = (part * N_B_GROUPS + g) * nh
        return pl.BlockSpec((1, s, HEAD_DIM), lambda bi, h: (bi, 0, base + h))

    def full_spec(arr):
        nd = arr.ndim
        return pl.BlockSpec(arr.shape, lambda bi, h: (0,) * nd)

    in_specs = [pl.BlockSpec(memory_space=pltpu.SMEM)]
    in_specs += [full_spec(bk) for bk in bks]
    in_specs += [head_spec(part, g) for part in range(3) for g in range(N_B_GROUPS)]
    in_specs += [pl.BlockSpec((1, HEAD_DIM), lambda bi, h: (0, 0))] * 2
    scratch = [pltpu.VMEM((N_B_GROUPS, s, HEAD_DIM), BF16)] * 3
    scratch += [pltpu.VMEM((N_B_GROUPS, s, HEAD_DIM), F32)] * 4
    scratch += [pltpu.VMEM(bk.shape, F32) for bk in bks]
    return pl.pallas_call(
        functools.partial(_attn_b_kernel, seq=s),
        grid=(b, nh),
        in_specs=in_specs,
        out_specs=pl.BlockSpec((1, s, HEAD_DIM), lambda bi, h: (bi, 0, h)),
        out_shape=jax.ShapeDtypeStruct((b, s, nh * HEAD_DIM), BF16),
        scratch_shapes=scratch,
        compiler_params=_cparams(("arbitrary", "arbitrary")),
        name="attention_b",
    )(rel_bias, *bks, *([qkv3] * 9), gq, gk)


def kernel(x, norm1_g, w_in, b_gate, q_norm_a, k_norm_a, q_norm_b, k_norm_b, rel_bias,
           w_proj_a, w_proj_b, w_out, norm2_g, w_ffn_gate, w_ffn_up, w_ffn_down):
    b, s, d = x.shape
    m = b * s
    a_w = (N_A_HEADS + 2 * N_A_KV_HEADS) * HEAD_DIM
    b_w = 3 * N_B_GROUPS * N_B_HEADS_PER_GROUP * HEAD_DIM
    cos, sin_signed = _rope_tables(s)
    d_ff = w_ffn_gate.shape[-1]
    tk_down = d_ff // 2

    xf = x.reshape(m, d)
    for l in range(norm1_g.shape[0]):
        h = _rmsnorm(xf, norm1_g[l], name="rmsnorm1")
        qkv_a = _matmul(h, w_in[l], col0=0, ncols=a_w, tm=1024, tn=512, out_dtype=BF16,
                        name="in_proj_a")
        qkv_b = _matmul(h, w_in[l], col0=a_w, ncols=b_w, tm=1024, tn=512, out_dtype=F32,
                        name="in_proj_b")
        gates = _matmul(h, w_in[l], col0=a_w + b_w, ncols=2 * d, tm=1024, tn=512, out_dtype=BF16,
                        name="in_proj_gates")

        o_a = _attention_a(qkv_a.reshape(b, s, a_w), cos, sin_signed, q_norm_a[l].reshape(1, -1),
                           k_norm_a[l].reshape(1, -1), tq=256)
        o_b = _attention_b(qkv_b.reshape(b, s, b_w), rel_bias, q_norm_b[l].reshape(1, -1),
                           k_norm_b[l].reshape(1, -1))

        merged = _merge(o_a.reshape(m, -1), o_b.reshape(m, -1), gates, b_gate[l], w_proj_a[l],
                        w_proj_b[l], tm=1024, tn=512, name="gated_merge")
        xf = _matmul_resid(merged, w_out[l], xf, tm=1024, tn=512, tk=d, kblock=0, name="out_proj")

        h2 = _rmsnorm(xf, norm2_g[l], name="rmsnorm2")
        act = _swiglu(h2, w_ffn_gate[l], w_ffn_up[l], tm=1024, tn=256, name="swiglu_up")
        for kblock in range(d_ff // tk_down):
            xf = _matmul_resid(act, w_ffn_down[l], xf, tm=512, tn=512, tk=tk_down, kblock=kblock,
                               name=f"ffn_down{kblock}")
    return xf.reshape(b, s, d)
```

```python
import functools
import math

import jax
import jax.numpy as jnp
import numpy as np
from jax import lax
from jax.experimental import pallas as pl
from jax.experimental.pallas import tpu as pltpu

F32 = jnp.float32
BF16 = jnp.bfloat16

HEAD_DIM = 128
N_A_HEADS = 16
N_A_KV_HEADS = 4
N_A_GROUP = N_A_HEADS // N_A_KV_HEADS
B_PATTERNS = ((128, 1), (512, 4), (2048, 16))
N_B_GROUPS = len(B_PATTERNS)
N_B_HEADS_PER_GROUP = 4
GRID_W = 64
ROPE_THETA = 10000.0
AXIS_ROPE_DIM = HEAD_DIM // 2
REL_BUCKETS = 32
REL_MAX_DIST = 1024
EPS = 1e-6
NEG_INF = -1e30
ATTN_SCALE = HEAD_DIM ** -0.5

Q_TILE = 128
HEAD_BLOCK = N_A_GROUP * HEAD_DIM
VMEM_LIMIT = 56 * 1024 * 1024


def _cparams(sem):
    return pltpu.CompilerParams(dimension_semantics=sem, vmem_limit_bytes=VMEM_LIMIT)


def _rmsnorm_kernel(x_ref, g_ref, o_ref):
    x = x_ref[...]
    ms = jnp.mean(x * x, axis=-1, keepdims=True)
    o_ref[...] = (x * lax.rsqrt(ms + EPS) * g_ref[...]).astype(o_ref.dtype)


def _rmsnorm(x, g, *, tm=256, name):
    m, d = x.shape
    return pl.pallas_call(
        _rmsnorm_kernel,
        grid=(m // tm,),
        in_specs=[pl.BlockSpec((tm, d), lambda i: (i, 0)),
                  pl.BlockSpec((1, d), lambda i: (0, 0))],
        out_specs=pl.BlockSpec((tm, d), lambda i: (i, 0)),
        out_shape=jax.ShapeDtypeStruct((m, d), BF16),
        compiler_params=_cparams(("parallel",)),
        name=name,
    )(x, g.reshape(1, d))


def _mm_kernel(a_ref, w_ref, o_ref, wb_ref):
    @pl.when(pl.program_id(1) == 0)
    def _():
        wb_ref[...] = w_ref[...].astype(BF16)

    o_ref[...] = jnp.dot(a_ref[...], wb_ref[...], preferred_element_type=F32).astype(o_ref.dtype)


def _matmul(a, w, *, col0, ncols, tm, tn, out_dtype, name):
    m, k = a.shape
    jb = col0 // tn
    return pl.pallas_call(
        _mm_kernel,
        grid=(ncols // tn, m // tm),
        in_specs=[pl.BlockSpec((tm, k), lambda j, i: (i, 0)),
                  pl.BlockSpec((k, tn), lambda j, i: (0, jb + j))],
        out_specs=pl.BlockSpec((tm, tn), lambda j, i: (i, j)),
        out_shape=jax.ShapeDtypeStruct((m, ncols), out_dtype),
        scratch_shapes=[pltpu.VMEM((k, tn), BF16)],
        compiler_params=_cparams(("arbitrary", "arbitrary")),
        name=name,
    )(a, w)


def _mm_resid_kernel(a_ref, w_ref, r_ref, o_ref, wb_ref):
    @pl.when(pl.program_id(1) == 0)
    def _():
        wb_ref[...] = w_ref[...].astype(BF16)

    o_ref[...] = r_ref[...] + jnp.dot(a_ref[...], wb_ref[...], preferred_element_type=F32)


def _matmul_resid(a, w, resid, *, tm, tn, tk, kblock, name):
    m = a.shape[0]
    n = w.shape[1]
    return pl.pallas_call(
        _mm_resid_kernel,
        grid=(n // tn, m // tm),
        in_specs=[pl.BlockSpec((tm, tk), lambda j, i: (i, kblock)),
                  pl.BlockSpec((tk, tn), lambda j, i: (kblock, j)),
                  pl.BlockSpec((tm, tn), lambda j, i: (i, j))],
        out_specs=pl.BlockSpec((tm, tn), lambda j, i: (i, j)),
        out_shape=jax.ShapeDtypeStruct((m, n), F32),
        scratch_shapes=[pltpu.VMEM((tk, tn), BF16)],
        compiler_params=_cparams(("arbitrary", "arbitrary")),
        name=name,
    )(a, w, resid)


def _merge_kernel(oa_ref, ob_ref, ga_ref, gb_ref, bg_ref, wa_ref, wb_ref, o_ref, was_ref, wbs_ref):
    @pl.when(pl.program_id(1) == 0)
    def _():
        was_ref[...] = wa_ref[...].astype(BF16)
        wbs_ref[...] = wb_ref[...].astype(BF16)

    pa = jnp.dot(oa_ref[...], was_ref[...], preferred_element_type=F32)
    pb = jnp.dot(ob_ref[...], wbs_ref[...], preferred_element_type=F32)
    gate_a = jax.nn.sigmoid(ga_ref[...].astype(F32) + bg_ref[0:1, :])
    gate_b = jax.nn.sigmoid(gb_ref[...].astype(F32) + bg_ref[1:2, :])
    o_ref[...] = (gate_a * pa + gate_b * pb).astype(o_ref.dtype)


def _merge(o_a, o_b, gates, b_gate, w_proj_a, w_proj_b, *, tm, tn, name):
    m, ka = o_a.shape
    kb = o_b.shape[1]
    n = w_proj_a.shape[1]
    gb_blk = n // tn
    return pl.pallas_call(
        _merge_kernel,
        grid=(n // tn, m // tm),
        in_specs=[pl.BlockSpec((tm, ka), lambda j, i: (i, 0)),
                  pl.BlockSpec((tm, kb), lambda j, i: (i, 0)),
                  pl.BlockSpec((tm, tn), lambda j, i: (i, j)),
                  pl.BlockSpec((tm, tn), lambda j, i: (i, gb_blk + j)),
                  pl.BlockSpec((2, tn), lambda j, i: (0, j)),
                  pl.BlockSpec((ka, tn), lambda j, i: (0, j)),
                  pl.BlockSpec((kb, tn), lambda j, i: (0, j))],
        out_specs=pl.BlockSpec((tm, tn), lambda j, i: (i, j)),
        out_shape=jax.ShapeDtypeStruct((m, n), BF16),
        scratch_shapes=[pltpu.VMEM((ka, tn), BF16), pltpu.VMEM((kb, tn), BF16)],
        compiler_params=_cparams(("arbitrary", "arbitrary")),
        name=name,
    )(o_a, o_b, gates, gates, b_gate, w_proj_a, w_proj_b)


def _swiglu_kernel(h_ref, wg_ref, wu_ref, o_ref, wgs_ref, wus_ref):
    @pl.when(pl.program_id(1) == 0)
    def _():
        wgs_ref[...] = wg_ref[...].astype(BF16)
        wus_ref[...] = wu_ref[...].astype(BF16)

    h = h_ref[...]
    g = jnp.dot(h, wgs_ref[...], preferred_element_type=F32)
    u = jnp.dot(h, wus_ref[...], preferred_element_type=F32)
    o_ref[...] = (g * jax.nn.sigmoid(g) * u).astype(o_ref.dtype)


def _swiglu(h, w_gate, w_up, *, tm, tn, name):
    m, k = h.shape
    n = w_gate.shape[1]
    return pl.pallas_call(
        _swiglu_kernel,
        grid=(n // tn, m // tm),
        in_specs=[pl.BlockSpec((tm, k), lambda j, i: (i, 0)),
                  pl.BlockSpec((k, tn), lambda j, i: (0, j)),
                  pl.BlockSpec((k, tn), lambda j, i: (0, j))],
        out_specs=pl.BlockSpec((tm, tn), lambda j, i: (i, j)),
        out_shape=jax.ShapeDtypeStruct((m, n), BF16),
        scratch_shapes=[pltpu.VMEM((k, tn), BF16), pltpu.VMEM((k, tn), BF16)],
        compiler_params=_cparams(("arbitrary", "arbitrary")),
        name=name,
    )(h, w_gate, w_up)


def _head_rms(x, g):
    ms = jnp.mean(x * x, axis=-1, keepdims=True)
    return x * lax.rsqrt(ms + EPS) * g


def _rope(x, cos, sin_signed):
    lane = lax.broadcasted_iota(jnp.int32, x.shape, 1)
    first_half = (lane % AXIS_ROPE_DIM) < (AXIS_ROPE_DIM // 2)
    partner = jnp.where(first_half,
                        pltpu.roll(x, HEAD_DIM - AXIS_ROPE_DIM // 2, 1),
                        pltpu.roll(x, AXIS_ROPE_DIM // 2, 1))
    return x * cos + partner * sin_signed


def _rope_tables(seq):
    rows = seq // GRID_W
    row = jnp.broadcast_to(jnp.arange(rows)[:, None], (rows, GRID_W)).reshape(-1).astype(F32)
    col = jnp.broadcast_to(jnp.arange(GRID_W)[None, :], (rows, GRID_W)).reshape(-1).astype(F32)
    inv = ROPE_THETA ** (-jnp.arange(0, AXIS_ROPE_DIM, 2, dtype=F32) / AXIS_ROPE_DIM)
    ang_row, ang_col = row[:, None] * inv, col[:, None] * inv
    cr, sr, cc, sc = jnp.cos(ang_row), jnp.sin(ang_row), jnp.cos(ang_col), jnp.sin(ang_col)
    cos = jnp.concatenate([cr, cr, cc, cc], axis=-1)
    sin_signed = jnp.concatenate([-sr, sr, -sc, sc], axis=-1)
    return cos, sin_signed


def _attn_a_kernel(q_ref, k_ref, v_ref, cq_ref, sq_ref, ck_ref, sk_ref, gq_ref, gk_ref,
                   o_ref, kn_ref):
    @pl.when(pl.program_id(2) == 0)
    def _():
        k = _head_rms(k_ref[0].astype(F32), gk_ref[...])
        kn_ref[...] = _rope(k, ck_ref[...], sk_ref[...]).astype(BF16)

    kn = kn_ref[...]
    v = v_ref[0]
    cq, sq, gq = cq_ref[...], sq_ref[...], gq_ref[...]
    for g in range(N_A_GROUP):
        cols = slice(g * HEAD_DIM, (g + 1) * HEAD_DIM)
        q = _head_rms(q_ref[0, :, cols].astype(F32), gq)
        q = (_rope(q, cq, sq) * ATTN_SCALE).astype(BF16)
        logits = lax.dot_general(q, kn, (((1,), (1,)), ((), ())), preferred_element_type=F32)
        m = jnp.max(logits, axis=-1, keepdims=True)
        p = jnp.exp(logits - m)
        denom = jnp.sum(p, axis=-1, keepdims=True)
        o = jnp.dot(p.astype(BF16), v, preferred_element_type=F32) / denom
        o_ref[0, :, cols] = o.astype(o_ref.dtype)


def _attention_a(qkv3, cos, sin_signed, gq, gk, *, tq):
    b, s, _ = qkv3.shape
    kb = N_A_HEADS
    vb = N_A_HEADS + N_A_KV_HEADS
    return pl.pallas_call(
        _attn_a_kernel,
        grid=(b, N_A_KV_HEADS, s // tq),
        in_specs=[pl.BlockSpec((1, tq, HEAD_BLOCK), lambda bi, kv, qi: (bi, qi, kv)),
                  pl.BlockSpec((1, s, HEAD_DIM), lambda bi, kv, qi: (bi, 0, kb + kv)),
                  pl.BlockSpec((1, s, HEAD_DIM), lambda bi, kv, qi: (bi, 0, vb + kv)),
                  pl.BlockSpec((tq, HEAD_DIM), lambda bi, kv, qi: (qi, 0)),
                  pl.BlockSpec((tq, HEAD_DIM), lambda bi, kv, qi: (qi, 0)),
                  pl.BlockSpec((s, HEAD_DIM), lambda bi, kv, qi: (0, 0)),
                  pl.BlockSpec((s, HEAD_DIM), lambda bi, kv, qi: (0, 0)),
                  pl.BlockSpec((1, HEAD_DIM), lambda bi, kv, qi: (0, 0)),
                  pl.BlockSpec((1, HEAD_DIM), lambda bi, kv, qi: (0, 0))],
        out_specs=pl.BlockSpec((1, tq, HEAD_BLOCK), lambda bi, kv, qi: (bi, qi, kv)),
        out_shape=jax.ShapeDtypeStruct((b, s, N_A_HEADS * HEAD_DIM), BF16),
        scratch_shapes=[pltpu.VMEM((s, HEAD_DIM), BF16)],
        compiler_params=_cparams(("arbitrary", "arbitrary", "arbitrary")),
        name="attention_a",
    )(qkv3, qkv3, qkv3, cos, sin_signed, cos, sin_signed, gq, gk)


RADIUS = 64
assert all(w // (2 * d) == RADIUS for w, d in B_PATTERNS)


def _t5_bucket_np(rel):
    nb = REL_BUCKETS // 2
    max_exact = nb // 2
    side = np.where(rel > 0, nb, 0)
    n = np.abs(rel)
    nf = np.maximum(n, 1).astype(np.float32)
    large = max_exact + (np.log(nf / np.float32(max_exact))
                         / np.float32(math.log(REL_MAX_DIST / max_exact))
                         * np.float32(nb - max_exact)).astype(np.int32)
    large = np.minimum(large, nb - 1)
    return (side + np.where(n < max_exact, n, large)).astype(np.int32)


def _band_geometry(seq, dil):
    length = seq // dil
    kw = min(2 * Q_TILE, length)
    nblk = length // Q_TILE
    starts = [min(max(mb * Q_TILE - RADIUS, 0), length - kw) for mb in range(nblk)]
    shifts = sorted({st - mb * Q_TILE for mb, st in enumerate(starts)}, reverse=True)
    return length, kw, nblk, shifts


def _bucket_tiles(seq, dil):
    _, kw, _, shifts = _band_geometry(seq, dil)
    i = np.arange(Q_TILE)[:, None]
    j = np.arange(kw)[None, :]
    tiles = []
    for shift in shifts:
        rel = j - i + shift
        tiles.append(np.where(np.abs(rel) <= RADIUS, _t5_bucket_np(rel * dil), -1))
    return np.stack(tiles, axis=0).astype(np.int32)


def _attn_b_kernel(rb_ref, bk0_ref, bk1_ref, bk2_ref,
                   q0_ref, q1_ref, q2_ref, k0_ref, k1_ref, k2_ref, v0_ref, v1_ref, v2_ref,
                   gq_ref, gk_ref, o_ref,
                   qc_ref, kc_ref, vc_ref, on_ref, ln_ref, t0_ref, t1_ref, t2_ref, *, seq):
    head = pl.program_id(0)
    gq, gk = gq_ref[...], gk_ref[...]
    bk_refs = (bk0_ref, bk1_ref, bk2_ref)
    tile_refs = (t0_ref, t1_ref, t2_ref)
    q_refs, k_refs, v_refs = (q0_ref, q1_ref, q2_ref), (k0_ref, k1_ref, k2_ref), (v0_ref, v1_ref, v2_ref)

    @pl.when(pl.program_id(1) == 0)
    def _():
        for g in range(N_B_GROUPS):
            col = g * N_B_HEADS_PER_GROUP + head
            for var in range(bk_refs[g].shape[0]):
                bkt = bk_refs[g][var]

                def pick(b, acc, bkt=bkt, col=col):
                    return jnp.where(bkt == b, rb_ref[b, col], acc)

                tile_refs[g][var] = lax.fori_loop(0, REL_BUCKETS, pick,
                                                  jnp.full(bkt.shape, NEG_INF, F32))

    for g, (_, dil) in enumerate(B_PATTERNS):
        length, kw, nblk, shifts = _band_geometry(seq, dil)

        for c in range(dil):
            rows = pl.ds(c, length, stride=dil) if dil > 1 else pl.ds(0, length)
            dst = pl.ds(c * length, length)
            qc_ref[g, dst, :] = (_head_rms(q_refs[g][0, rows, :], gq) * ATTN_SCALE).astype(BF16)
            kc_ref[g, dst, :] = _head_rms(k_refs[g][0, rows, :], gk).astype(BF16)
            vc_ref[g, dst, :] = v_refs[g][0, rows, :].astype(BF16)

        for c in range(dil):
            for mb in range(nblk):
                start = min(max(mb * Q_TILE - RADIUS, 0), length - kw)
                var = shifts.index(start - mb * Q_TILE)
                q = qc_ref[g, pl.ds(c * length + mb * Q_TILE, Q_TILE), :]
                k = kc_ref[g, pl.ds(c * length + start, kw), :]
                v = vc_ref[g, pl.ds(c * length + start, kw), :]
                logits = lax.dot_general(q, k, (((1,), (1,)), ((), ())), preferred_element_type=F32)
                logits = logits + tile_refs[g][var]
                m = jnp.max(logits, axis=-1, keepdims=True)
                p = jnp.exp(logits - m)
                denom = jnp.sum(p, axis=-1, keepdims=True)
                o = jnp.dot(p.astype(BF16), v, preferred_element_type=F32) / denom
                t0 = c + dil * mb * Q_TILE
                rows = pl.ds(t0, Q_TILE, stride=dil) if dil > 1 else pl.ds(t0, Q_TILE)
                on_ref[g, rows, :] = o
                ln_ref[g, rows, :] = jnp.broadcast_to(m + jnp.log(denom), (Q_TILE, HEAD_DIM))

    chunk = 2 * Q_TILE

    def combine(i, carry):
        rows = pl.ds(pl.multiple_of(i * chunk, chunk), chunk)
        l0, l1, l2 = ln_ref[0, rows, :], ln_ref[1, rows, :], ln_ref[2, rows, :]
        mx = jnp.maximum(jnp.maximum(l0, l1), l2)
        e0, e1, e2 = jnp.exp(l0 - mx), jnp.exp(l1 - mx), jnp.exp(l2 - mx)
        num = e0 * on_ref[0, rows, :] + e1 * on_ref[1, rows, :] + e2 * on_ref[2, rows, :]
        o_ref[0, rows, :] = (num / (e0 + e1 + e2)).astype(o_ref.dtype)
        return carry

    lax.fori_loop(0, seq // chunk, combine, 0)


def _attention_b(qkv3, rel_bias, gq, gk):
    b, s, _ = qkv3.shape
    nh = N_B_HEADS_PER_GROUP
    bks = [jnp.asarray(_bucket_tiles(s, dil)) for _, dil in B_PATTERNS]

    def head_spec(part, g):
        base = (part * N_B_GROUPS + g) * nh
        return pl.BlockSpec((1, s, HEAD_DIM), lambda h, bi: (bi, 0, base + h))

    def full_spec(arr):
        nd = arr.ndim
        return pl.BlockSpec(arr.shape, lambda h, bi: (0,) * nd)

    in_specs = [pl.BlockSpec(memory_space=pltpu.SMEM)]
    in_specs += [full_spec(bk) for bk in bks]
    in_specs += [head_spec(part, g) for part in range(3) for g in range(N_B_GROUPS)]
    in_specs += [pl.BlockSpec((1, HEAD_DIM), lambda h, bi: (0, 0))] * 2
    scratch = [pltpu.VMEM((N_B_GROUPS, s, HEAD_DIM), BF16)] * 3
    scratch += [pltpu.VMEM((N_B_GROUPS, s, HEAD_DIM), F32)] * 2
    scratch += [pltpu.VMEM(bk.shape, F32) for bk in bks]
    return pl.pallas_call(
        functools.partial(_attn_b_kernel, seq=s),
        grid=(nh, b),
        in_specs=in_specs,
        out_specs=pl.BlockSpec((1, s, HEAD_DIM), lambda h, bi: (bi, 0, h)),
        out_shape=jax.ShapeDtypeStruct((b, s, nh * HEAD_DIM), BF16),
        scratch_shapes=scratch,
        compiler_params=_cparams(("arbitrary", "arbitrary")),
        name="attention_b",
    )(rel_bias, *bks, *([qkv3] * 9), gq, gk)


def kernel(x, norm1_g, w_in, b_gate, q_norm_a, k_norm_a, q_norm_b, k_norm_b, rel_bias,
           w_proj_a, w_proj_b, w_out, norm2_g, w_ffn_gate, w_ffn_up, w_ffn_down):
    b, s, d = x.shape
    m = b * s
    a_w = (N_A_HEADS + 2 * N_A_KV_HEADS) * HEAD_DIM
    b_w = 3 * N_B_GROUPS * N_B_HEADS_PER_GROUP * HEAD_DIM
    cos, sin_signed = _rope_tables(s)
    d_ff = w_ffn_gate.shape[-1]
    tk_down = d_ff // 2

    xf = x.reshape(m, d)
    for l in range(norm1_g.shape[0]):
        h = _rmsnorm(xf, norm1_g[l], name="rmsnorm1")
        qkv_a = _matmul(h, w_in[l], col0=0, ncols=a_w, tm=1024, tn=512, out_dtype=BF16,
                        name="in_proj_a")
        qkv_b = _matmul(h, w_in[l], col0=a_w, ncols=b_w, tm=1024, tn=512, out_dtype=F32,
                        name="in_proj_b")
        gates = _matmul(h, w_in[l], col0=a_w + b_w, ncols=2 * d, tm=1024, tn=512, out_dtype=BF16,
                        name="in_proj_gates")

        o_a = _attention_a(qkv_a.reshape(b, s, a_w), cos, sin_signed, q_norm_a[l].reshape(1, -1),
                           k_norm_a[l].reshape(1, -1), tq=256)
        o_b = _attention_b(qkv_b.reshape(b, s, b_w), rel_bias, q_norm_b[l].reshape(1, -1),
                           k_norm_b[l].reshape(1, -1))

        merged = _merge(o_a.reshape(m, -1), o_b.reshape(m, -1), gates, b_gate[l], w_proj_a[l],
                        w_proj_b[l], tm=1024, tn=512, name="gated_merge")
        xf = _matmul_resid(merged, w_out[l], xf, tm=1024, tn=512, tk=d, kblock=0, name="out_proj")

        h2 = _rmsnorm(xf, norm2_g[l], name="rmsnorm2")
        act = _swiglu(h2, w_ffn_gate[l], w_ffn_up[l], tm=1024, tn=256, name="swiglu_up")
        for kblock in range(d_ff // tk_down):
            xf = _matmul_resid(act, w_ffn_down[l], xf, tm=512, tn=512, tk=tk_down, kblock=kblock,
                               name=f"ffn_down{kblock}")
    return xf.reshape(b, s, d)
```

```python
import functools
import math

import jax
import jax.numpy as jnp
import numpy as np
from jax import lax
from jax.experimental import pallas as pl
from jax.experimental.pallas import tpu as pltpu

F32 = jnp.float32
BF16 = jnp.bfloat16

HEAD_DIM = 128
N_A_HEADS = 16
N_A_KV_HEADS = 4
N_A_GROUP = N_A_HEADS // N_A_KV_HEADS
B_PATTERNS = ((128, 1), (512, 4), (2048, 16))
N_B_GROUPS = len(B_PATTERNS)
N_B_HEADS_PER_GROUP = 4
GRID_W = 64
ROPE_THETA = 10000.0
AXIS_ROPE_DIM = HEAD_DIM // 2
REL_BUCKETS = 32
REL_MAX_DIST = 1024
EPS = 1e-6
NEG_INF = -1e30
ATTN_SCALE = HEAD_DIM ** -0.5

Q_TILE = 128
HEAD_BLOCK = N_A_GROUP * HEAD_DIM
VMEM_LIMIT = 56 * 1024 * 1024


def _cparams(sem):
    return pltpu.CompilerParams(dimension_semantics=sem, vmem_limit_bytes=VMEM_LIMIT)


def _rmsnorm_kernel(x_ref, g_ref, o_ref):
    x = x_ref[...]
    ms = jnp.mean(x * x, axis=-1, keepdims=True)
    o_ref[...] = (x * lax.rsqrt(ms + EPS) * g_ref[...]).astype(o_ref.dtype)


def _rmsnorm(x, g, *, tm=256, name):
    m, d = x.shape
    return pl.pallas_call(
        _rmsnorm_kernel,
        grid=(m // tm,),
        in_specs=[pl.BlockSpec((tm, d), lambda i: (i, 0)),
                  pl.BlockSpec((1, d), lambda i: (0, 0))],
        out_specs=pl.BlockSpec((tm, d), lambda i: (i, 0)),
        out_shape=jax.ShapeDtypeStruct((m, d), BF16),
        compiler_params=_cparams(("parallel",)),
        name=name,
    )(x, g.reshape(1, d))


def _mm_kernel(a_ref, w_ref, o_ref, wb_ref):
    @pl.when(pl.program_id(1) == 0)
    def _():
        wb_ref[...] = w_ref[...].astype(BF16)

    o_ref[...] = jnp.dot(a_ref[...], wb_ref[...], preferred_element_type=F32).astype(o_ref.dtype)


def _matmul(a, w, *, col0, ncols, tm, tn, out_dtype, name):
    m, k = a.shape
    jb = col0 // tn
    return pl.pallas_call(
        _mm_kernel,
        grid=(ncols // tn, m // tm),
        in_specs=[pl.BlockSpec((tm, k), lambda j, i: (i, 0)),
                  pl.BlockSpec((k, tn), lambda j, i: (0, jb + j))],
        out_specs=pl.BlockSpec((tm, tn), lambda j, i: (i, j)),
        out_shape=jax.ShapeDtypeStruct((m, ncols), out_dtype),
        scratch_shapes=[pltpu.VMEM((k, tn), BF16)],
        compiler_params=_cparams(("arbitrary", "arbitrary")),
        name=name,
    )(a, w)


def _mm_resid_kernel(a_ref, w_ref, r_ref, o_ref, wb_ref):
    @pl.when(pl.program_id(1) == 0)
    def _():
        wb_ref[...] = w_ref[...].astype(BF16)

    o_ref[...] = r_ref[...] + jnp.dot(a_ref[...], wb_ref[...], preferred_element_type=F32)


def _matmul_resid(a, w, resid, *, tm, tn, tk, kblock, name):
    m = a.shape[0]
    n = w.shape[1]
    return pl.pallas_call(
        _mm_resid_kernel,
        grid=(n // tn, m // tm),
        in_specs=[pl.BlockSpec((tm, tk), lambda j, i: (i, kblock)),
                  pl.BlockSpec((tk, tn), lambda j, i: (kblock, j)),
                  pl.BlockSpec((tm, tn), lambda j, i: (i, j))],
        out_specs=pl.BlockSpec((tm, tn), lambda j, i: (i, j)),
        out_shape=jax.ShapeDtypeStruct((m, n), F32),
        scratch_shapes=[pltpu.VMEM((tk, tn), BF16)],
        compiler_params=_cparams(("arbitrary", "arbitrary")),
        name=name,
    )(a, w, resid)


def _merge_kernel(oa_ref, ob_ref, ga_ref, gb_ref, bg_ref, wa_ref, wb_ref, o_ref, was_ref, wbs_ref):
    @pl.when(pl.program_id(1) == 0)
    def _():
        was_ref[...] = wa_ref[...].astype(BF16)
        wbs_ref[...] = wb_ref[...].astype(BF16)

    pa = jnp.dot(oa_ref[...], was_ref[...], preferred_element_type=F32)
    pb = jnp.dot(ob_ref[...], wbs_ref[...], preferred_element_type=F32)
    gate_a = jax.nn.sigmoid(ga_ref[...].astype(F32) + bg_ref[0:1, :])
    gate_b = jax.nn.sigmoid(gb_ref[...].astype(F32) + bg_ref[1:2, :])
    o_ref[...] = (gate_a * pa + gate_b * pb).astype(o_ref.dtype)


def _merge(o_a, o_b, gates, b_gate, w_proj_a, w_proj_b, *, tm, tn, name):
    m, ka = o_a.shape
    kb = o_b.shape[1]
    n = w_proj_a.shape[1]
    gb_blk = n // tn
    return pl.pallas_call(
        _merge_kernel,
        grid=(n // tn, m // tm),
        in_specs=[pl.BlockSpec((tm, ka), lambda j, i: (i, 0)),
                  pl.BlockSpec((tm, kb), lambda j, i: (i, 0)),
                  pl.BlockSpec((tm, tn), lambda j, i: (i, j)),
                  pl.BlockSpec((tm, tn), lambda j, i: (i, gb_blk + j)),
                  pl.BlockSpec((2, tn), lambda j, i: (0, j)),
                  pl.BlockSpec((ka, tn), lambda j, i: (0, j)),
                  pl.BlockSpec((kb, tn), lambda j, i: (0, j))],
        out_specs=pl.BlockSpec((tm, tn), lambda j, i: (i, j)),
        out_shape=jax.ShapeDtypeStruct((m, n), BF16),
        scratch_shapes=[pltpu.VMEM((ka, tn), BF16), pltpu.VMEM((kb, tn), BF16)],
        compiler_params=_cparams(("arbitrary", "arbitrary")),
        name=name,
    )(o_a, o_b, gates, gates, b_gate, w_proj_a, w_proj_b)


def _swiglu_kernel(h_ref, wg_ref, wu_ref, o_ref, wgs_ref, wus_ref):
    @pl.when(pl.program_id(1) == 0)
    def _():
        wgs_ref[...] = wg_ref[...].astype(BF16)
        wus_ref[...] = wu_ref[...].astype(BF16)

    h = h_ref[...]
    g = jnp.dot(h, wgs_ref[...], preferred_element_type=F32)
    u = jnp.dot(h, wus_ref[...], preferred_element_type=F32)
    o_ref[...] = (g * jax.nn.sigmoid(g) * u).astype(o_ref.dtype)


def _swiglu(h, w_gate, w_up, *, tm, tn, name):
    m, k = h.shape
    n = w_gate.shape[1]
    return pl.pallas_call(
        _swiglu_kernel,
        grid=(n // tn, m // tm),
        in_specs=[pl.BlockSpec((tm, k), lambda j, i: (i, 0)),
                  pl.BlockSpec((k, tn), lambda j, i: (0, j)),
                  pl.BlockSpec((k, tn), lambda j, i: (0, j))],
        out_specs=pl.BlockSpec((tm, tn), lambda j, i: (i, j)),
        out_shape=jax.ShapeDtypeStruct((m, n), BF16),
        scratch_shapes=[pltpu.VMEM((k, tn), BF16), pltpu.VMEM((k, tn), BF16)],
        compiler_params=_cparams(("arbitrary", "arbitrary")),
        name=name,
    )(h, w_gate, w_up)


def _head_rms(x, g):
    ms = jnp.mean(x * x, axis=-1, keepdims=True)
    return x * lax.rsqrt(ms + EPS) * g


def _rope(x, cos, sin_signed):
    lane = lax.broadcasted_iota(jnp.int32, x.shape, 1)
    first_half = (lane % AXIS_ROPE_DIM) < (AXIS_ROPE_DIM // 2)
    partner = jnp.where(first_half,
                        pltpu.roll(x, HEAD_DIM - AXIS_ROPE_DIM // 2, 1),
                        pltpu.roll(x, AXIS_ROPE_DIM // 2, 1))
    return x * cos + partner * sin_signed


def _rope_tables(seq):
    rows = seq // GRID_W
    row = jnp.broadcast_to(jnp.arange(rows)[:, None], (rows, GRID_W)).reshape(-1).astype(F32)
    col = jnp.broadcast_to(jnp.arange(GRID_W)[None, :], (rows, GRID_W)).reshape(-1).astype(F32)
    inv = ROPE_THETA ** (-jnp.arange(0, AXIS_ROPE_DIM, 2, dtype=F32) / AXIS_ROPE_DIM)
    ang_row, ang_col = row[:, None] * inv, col[:, None] * inv
    cr, sr, cc, sc = jnp.cos(ang_row), jnp.sin(ang_row), jnp.cos(ang_col), jnp.sin(ang_col)
    cos = jnp.concatenate([cr, cr, cc, cc], axis=-1)
    sin_signed = jnp.concatenate([-sr, sr, -sc, sc], axis=-1)
    return cos, sin_signed


SOFTMAX_ROWS = 16
LOG2E = math.log2(math.e)


def _attn_a_kernel(q_ref, k_ref, v_ref, cq_ref, sq_ref, ck_ref, sk_ref, gq_ref, gk_ref,
                   o_ref, kn_ref, v1_ref, s_ref, p_ref):
    @pl.when(pl.program_id(2) == 0)
    def _():
        k = _head_rms(k_ref[0].astype(F32), gk_ref[...])
        kn_ref[...] = _rope(k, ck_ref[...], sk_ref[...]).astype(BF16)
        v1_ref[:, :HEAD_DIM] = v_ref[0]
        v1_ref[:, HEAD_DIM:] = jnp.ones((v_ref.shape[1], HEAD_DIM), BF16)

    kn = kn_ref[...]
    v1 = v1_ref[...]
    cq, sq, gq = cq_ref[...], sq_ref[...], gq_ref[...]
    tq = q_ref.shape[1]
    for g in range(N_A_GROUP):
        cols = slice(g * HEAD_DIM, (g + 1) * HEAD_DIM)
        slot = g % 2
        q = _head_rms(q_ref[0, :, cols].astype(F32), gq)
        q = (_rope(q, cq, sq) * (ATTN_SCALE * LOG2E)).astype(BF16)
        s_ref[slot] = lax.dot_general(q, kn, (((1,), (1,)), ((), ())), preferred_element_type=F32)
        for r in range(tq // SOFTMAX_ROWS):
            rows = pl.ds(r * SOFTMAX_ROWS, SOFTMAX_ROWS)
            s = s_ref[slot, rows, :]
            p_ref[slot, rows, :] = jnp.exp2(s - jnp.max(s, axis=-1, keepdims=True)).astype(BF16)
        ov = jnp.dot(p_ref[slot], v1, preferred_element_type=F32)
        o_ref[0, :, cols] = (ov[:, :HEAD_DIM] / ov[:, HEAD_DIM:]).astype(o_ref.dtype)


def _attention_a(qkv3, cos, sin_signed, gq, gk, *, tq):
    b, s, _ = qkv3.shape
    kb = N_A_HEADS
    vb = N_A_HEADS + N_A_KV_HEADS
    return pl.pallas_call(
        _attn_a_kernel,
        grid=(b, N_A_KV_HEADS, s // tq),
        in_specs=[pl.BlockSpec((1, tq, HEAD_BLOCK), lambda bi, kv, qi: (bi, qi, kv)),
                  pl.BlockSpec((1, s, HEAD_DIM), lambda bi, kv, qi: (bi, 0, kb + kv)),
                  pl.BlockSpec((1, s, HEAD_DIM), lambda bi, kv, qi: (bi, 0, vb + kv)),
                  pl.BlockSpec((tq, HEAD_DIM), lambda bi, kv, qi: (qi, 0)),
                  pl.BlockSpec((tq, HEAD_DIM), lambda bi, kv, qi: (qi, 0)),
                  pl.BlockSpec((s, HEAD_DIM), lambda bi, kv, qi: (0, 0)),
                  pl.BlockSpec((s, HEAD_DIM), lambda bi, kv, qi: (0, 0)),
                  pl.BlockSpec((1, HEAD_DIM), lambda bi, kv, qi: (0, 0)),
                  pl.BlockSpec((1, HEAD_DIM), lambda bi, kv, qi: (0, 0))],
        out_specs=pl.BlockSpec((1, tq, HEAD_BLOCK), lambda bi, kv, qi: (bi, qi, kv)),
        out_shape=jax.ShapeDtypeStruct((b, s, N_A_HEADS * HEAD_DIM), BF16),
        scratch_shapes=[pltpu.VMEM((s, HEAD_DIM), BF16),
                        pltpu.VMEM((s, 2 * HEAD_DIM), BF16),
                        pltpu.VMEM((2, tq, s), F32),
                        pltpu.VMEM((2, tq, s), BF16)],
        compiler_params=_cparams(("arbitrary", "arbitrary", "arbitrary")),
        name="attention_a",
    )(qkv3, qkv3, qkv3, cos, sin_signed, cos, sin_signed, gq, gk)


RADIUS = 64
assert all(w // (2 * d) == RADIUS for w, d in B_PATTERNS)


def _t5_bucket_np(rel):
    nb = REL_BUCKETS // 2
    max_exact = nb // 2
    side = np.where(rel > 0, nb, 0)
    n = np.abs(rel)
    nf = np.maximum(n, 1).astype(np.float32)
    large = max_exact + (np.log(nf / np.float32(max_exact))
                         / np.float32(math.log(REL_MAX_DIST / max_exact))
                         * np.float32(nb - max_exact)).astype(np.int32)
    large = np.minimum(large, nb - 1)
    return (side + np.where(n < max_exact, n, large)).astype(np.int32)


def _band_geometry(seq, dil):
    length = seq // dil
    kw = min(2 * Q_TILE, length)
    nblk = length // Q_TILE
    starts = [min(max(mb * Q_TILE - RADIUS, 0), length - kw) for mb in range(nblk)]
    shifts = sorted({st - mb * Q_TILE for mb, st in enumerate(starts)}, reverse=True)
    return length, kw, nblk, shifts


def _bucket_tiles(seq, dil):
    _, kw, _, shifts = _band_geometry(seq, dil)
    i = np.arange(Q_TILE)[:, None]
    j = np.arange(kw)[None, :]
    tiles = []
    for shift in shifts:
        rel = j - i + shift
        tiles.append(np.where(np.abs(rel) <= RADIUS, _t5_bucket_np(rel * dil), -1))
    return np.stack(tiles, axis=0).astype(np.int32)


def _attn_b_kernel(rb_ref, bk0_ref, bk1_ref, bk2_ref,
                   q0_ref, q1_ref, q2_ref, k0_ref, k1_ref, k2_ref, v0_ref, v1_ref, v2_ref,
                   gq_ref, gk_ref, o_ref,
                   qc_ref, kc_ref, vc_ref, on_ref, ln_ref, t0_ref, t1_ref, t2_ref, *, seq):
    head = pl.program_id(0)
    gq, gk = gq_ref[...], gk_ref[...]
    bk_refs = (bk0_ref, bk1_ref, bk2_ref)
    tile_refs = (t0_ref, t1_ref, t2_ref)
    q_refs, k_refs, v_refs = (q0_ref, q1_ref, q2_ref), (k0_ref, k1_ref, k2_ref), (v0_ref, v1_ref, v2_ref)

    @pl.when(pl.program_id(1) == 0)
    def _():
        for g in range(N_B_GROUPS):
            col = g * N_B_HEADS_PER_GROUP + head
            for var in range(bk_refs[g].shape[0]):
                bkt = bk_refs[g][var]

                def pick(b, acc, bkt=bkt, col=col):
                    return jnp.where(bkt == b, rb_ref[b, col], acc)

                tile_refs[g][var] = lax.fori_loop(0, REL_BUCKETS, pick,
                                                  jnp.full(bkt.shape, NEG_INF, F32))

    for g, (_, dil) in enumerate(B_PATTERNS):
        length, kw, nblk, shifts = _band_geometry(seq, dil)

        for c in range(dil):
            rows = pl.ds(c, length, stride=dil) if dil > 1 else pl.ds(0, length)
            dst = pl.ds(c * length, length)
            qc_ref[g, dst, :] = (_head_rms(q_refs[g][0, rows, :], gq) * ATTN_SCALE).astype(BF16)
            kc_ref[g, dst, :] = _head_rms(k_refs[g][0, rows, :], gk).astype(BF16)
            vc_ref[g, dst, :] = v_refs[g][0, rows, :].astype(BF16)

        for c in range(dil):
            for mb in range(nblk):
                start = min(max(mb * Q_TILE - RADIUS, 0), length - kw)
                var = shifts.index(start - mb * Q_TILE)
                q = qc_ref[g, pl.ds(c * length + mb * Q_TILE, Q_TILE), :]
                k = kc_ref[g, pl.ds(c * length + start, kw), :]
                v = vc_ref[g, pl.ds(c * length + start, kw), :]
                logits = lax.dot_general(q, k, (((1,), (1,)), ((), ())), preferred_element_type=F32)
                logits = logits + tile_refs[g][var]
                m = jnp.max(logits, axis=-1, keepdims=True)
                p = jnp.exp(logits - m)
                denom = jnp.sum(p, axis=-1, keepdims=True)
                o = jnp.dot(p.astype(BF16), v, preferred_element_type=F32) / denom
                t0 = c + dil * mb * Q_TILE
                rows = pl.ds(t0, Q_TILE, stride=dil) if dil > 1 else pl.ds(t0, Q_TILE)
                on_ref[g, rows, :] = o
                ln_ref[g, rows, :] = jnp.broadcast_to(m + jnp.log(denom), (Q_TILE, HEAD_DIM))

    chunk = 2 * Q_TILE

    def combine(i, carry):
        rows = pl.ds(pl.multiple_of(i * chunk, chunk), chunk)
        l0, l1, l2 = ln_ref[0, rows, :], ln_ref[1, rows, :], ln_ref[2, rows, :]
        mx = jnp.maximum(jnp.maximum(l0, l1), l2)
        e0, e1, e2 = jnp.exp(l0 - mx), jnp.exp(l1 - mx), jnp.exp(l2 - mx)
        num = e0 * on_ref[0, rows, :] + e1 * on_ref[1, rows, :] + e2 * on_ref[2, rows, :]
        o_ref[0, rows, :] = (num / (e0 + e1 + e2)).astype(o_ref.dtype)
        return carry

    lax.fori_loop(0, seq // chunk, combine, 0)


def _attention_b(qkv3, rel_bias, gq, gk):
    b, s, _ = qkv3.shape
    nh = N_B_HEADS_PER_GROUP
    bks = [jnp.asarray(_bucket_tiles(s, dil)) for _, dil in B_PATTERNS]

    def head_spec(part, g):
        base = (part * N_B_GROUPS + g) * nh
        return pl.BlockSpec((1, s, HEAD_DIM), lambda h, bi: (bi, 0, base + h))

    def full_spec(arr):
        nd = arr.ndim
        return pl.BlockSpec(arr.shape, lambda h, bi: (0,) * nd)

    in_specs = [pl.BlockSpec(memory_space=pltpu.SMEM)]
    in_specs += [full_spec(bk) for bk in bks]
    in_specs += [head_spec(part, g) for part in range(3) for g in range(N_B_GROUPS)]
    in_specs += [pl.BlockSpec((1, HEAD_DIM), lambda h, bi: (0, 0))] * 2
    scratch = [pltpu.VMEM((N_B_GROUPS, s, HEAD_DIM), BF16)] * 3
    scratch += [pltpu.VMEM((N_B_GROUPS, s, HEAD_DIM), F32)] * 2
    scratch += [pltpu.VMEM(bk.shape, F32) for bk in bks]
    return pl.pallas_call(
        functools.partial(_attn_b_kernel, seq=s),
        grid=(nh, b),
        in_specs=in_specs,
        out_specs=pl.BlockSpec((1, s, HEAD_DIM), lambda h, bi: (bi, 0, h)),
        out_shape=jax.ShapeDtypeStruct((b, s, nh * HEAD_DIM), BF16),
        scratch_shapes=scratch,
        compiler_params=_cparams(("arbitrary", "arbitrary")),
        name="attention_b",
    )(rel_bias, *bks, *([qkv3] * 9), gq, gk)


def kernel(x, norm1_g, w_in, b_gate, q_norm_a, k_norm_a, q_norm_b, k_norm_b, rel_bias,
           w_proj_a, w_proj_b, w_out, norm2_g, w_ffn_gate, w_ffn_up, w_ffn_down):
    b, s, d = x.shape
    m = b * s
    a_w = (N_A_HEADS + 2 * N_A_KV_HEADS) * HEAD_DIM
    b_w = 3 * N_B_GROUPS * N_B_HEADS_PER_GROUP * HEAD_DIM
    cos, sin_signed = _rope_tables(s)
    d_ff = w_ffn_gate.shape[-1]
    tk_down = d_ff // 2

    xf = x.reshape(m, d)
    for l in range(norm1_g.shape[0]):
        h = _rmsnorm(xf, norm1_g[l], name="rmsnorm1")
        qkv_a = _matmul(h, w_in[l], col0=0, ncols=a_w, tm=512, tn=512, out_dtype=BF16,
                        name="in_proj_a")
        qkv_b = _matmul(h, w_in[l], col0=a_w, ncols=b_w, tm=1024, tn=512, out_dtype=F32,
                        name="in_proj_b")
        gates = _matmul(h, w_in[l], col0=a_w + b_w, ncols=2 * d, tm=1024, tn=512, out_dtype=BF16,
                        name="in_proj_gates")

        o_a = _attention_a(qkv_a.reshape(b, s, a_w), cos, sin_signed, q_norm_a[l].reshape(1, -1),
                           k_norm_a[l].reshape(1, -1), tq=256)
        o_b = _attention_b(qkv_b.reshape(b, s, b_w), rel_bias, q_norm_b[l].reshape(1, -1),
                           k_norm_b[l].reshape(1, -1))

        merged = _merge(o_a.reshape(m, -1), o_b.reshape(m, -1), gates, b_gate[l], w_proj_a[l],
                        w_proj_b[l], tm=1024, tn=512, name="gated_merge")
        xf = _matmul_resid(merged, w_out[l], xf, tm=1024, tn=512, tk=d, kblock=0, name="out_proj")

        h2 = _rmsnorm(xf, norm2_g[l], name="rmsnorm2")
        act = _swiglu(h2, w_ffn_gate[l], w_ffn_up[l], tm=1024, tn=256, name="swiglu_up")
        for kblock in range(d_ff // tk_down):
            xf = _matmul_resid(act, w_ffn_down[l], xf, tm=512, tn=512, tk=tk_down, kblock=kblock,
                               name=f"ffn_down{kblock}")
    return xf.reshape(b, s, d)
```

```python
import functools
import math

import jax
import jax.numpy as jnp
import numpy as np
from jax import lax
from jax.experimental import pallas as pl
from jax.experimental.pallas import tpu as pltpu

F32 = jnp.float32
BF16 = jnp.bfloat16

HEAD_DIM = 128
N_A_HEADS = 16
N_A_KV_HEADS = 4
N_A_GROUP = N_A_HEADS // N_A_KV_HEADS
B_PATTERNS = ((128, 1), (512, 4), (2048, 16))
N_B_GROUPS = len(B_PATTERNS)
N_B_HEADS_PER_GROUP = 4
GRID_W = 64
ROPE_THETA = 10000.0
AXIS_ROPE_DIM = HEAD_DIM // 2
REL_BUCKETS = 32
REL_MAX_DIST = 1024
EPS = 1e-6
NEG_INF = -1e30
ATTN_SCALE = HEAD_DIM ** -0.5

Q_TILE = 128
HEAD_BLOCK = N_A_GROUP * HEAD_DIM
VMEM_LIMIT = 56 * 1024 * 1024


def _cparams(sem):
    return pltpu.CompilerParams(dimension_semantics=sem, vmem_limit_bytes=VMEM_LIMIT)


def _rmsnorm_kernel(x_ref, g_ref, o_ref):
    x = x_ref[...]
    ms = jnp.mean(x * x, axis=-1, keepdims=True)
    o_ref[...] = (x * lax.rsqrt(ms + EPS) * g_ref[...]).astype(o_ref.dtype)


def _rmsnorm(x, g, *, tm=256, name):
    m, d = x.shape
    return pl.pallas_call(
        _rmsnorm_kernel,
        grid=(m // tm,),
        in_specs=[pl.BlockSpec((tm, d), lambda i: (i, 0)),
                  pl.BlockSpec((1, d), lambda i: (0, 0))],
        out_specs=pl.BlockSpec((tm, d), lambda i: (i, 0)),
        out_shape=jax.ShapeDtypeStruct((m, d), BF16),
        compiler_params=_cparams(("parallel",)),
        name=name,
    )(x, g.reshape(1, d))


def _row_spec(tm, k, single_buffered=False):
    if single_buffered:
        return pl.BlockSpec((tm, k), lambda i, j: (i, 0), pipeline_mode=pl.Buffered(1))
    return pl.BlockSpec((tm, k), lambda i, j: (i, 0))


def _mm_kernel(a_ref, w_ref, o_ref):
    w = w_ref[...].astype(BF16)
    o_ref[...] = jnp.dot(a_ref[...], w, preferred_element_type=F32).astype(o_ref.dtype)


def _matmul(a, w, *, col0, ncols, tm, tn, out_dtype, name):
    m, k = a.shape
    jb = col0 // tn
    return pl.pallas_call(
        _mm_kernel,
        grid=(m // tm, ncols // tn),
        in_specs=[_row_spec(tm, k),
                  pl.BlockSpec((k, tn), lambda i, j: (0, jb + j))],
        out_specs=pl.BlockSpec((tm, tn), lambda i, j: (i, j)),
        out_shape=jax.ShapeDtypeStruct((m, ncols), out_dtype),
        compiler_params=_cparams(("arbitrary", "arbitrary")),
        name=name,
    )(a, w)


def _mm_resid_rows_kernel(a_ref, w_ref, r_ref, o_ref):
    w = w_ref[...].astype(BF16)
    o_ref[...] = r_ref[...] + jnp.dot(a_ref[...], w, preferred_element_type=F32)


def _matmul_resid_rows(a, w, resid, *, tm, tn, name):
    m, k = a.shape
    n = w.shape[1]
    return pl.pallas_call(
        _mm_resid_rows_kernel,
        grid=(m // tm, n // tn),
        in_specs=[_row_spec(tm, k, single_buffered=True),
                  pl.BlockSpec((k, tn), lambda i, j: (0, j)),
                  pl.BlockSpec((tm, tn), lambda i, j: (i, j))],
        out_specs=pl.BlockSpec((tm, tn), lambda i, j: (i, j)),
        out_shape=jax.ShapeDtypeStruct((m, n), F32),
        compiler_params=_cparams(("arbitrary", "arbitrary")),
        name=name,
    )(a, w, resid)


def _mm_resid_kernel(a_ref, w_ref, r_ref, o_ref, wb_ref):
    @pl.when(pl.program_id(1) == 0)
    def _():
        wb_ref[...] = w_ref[...].astype(BF16)

    o_ref[...] = r_ref[...] + jnp.dot(a_ref[...], wb_ref[...], preferred_element_type=F32)


def _matmul_resid(a, w, resid, *, tm, tn, tk, kblock, name):
    m = a.shape[0]
    n = w.shape[1]
    return pl.pallas_call(
        _mm_resid_kernel,
        grid=(n // tn, m // tm),
        in_specs=[pl.BlockSpec((tm, tk), lambda j, i: (i, kblock)),
                  pl.BlockSpec((tk, tn), lambda j, i: (kblock, j)),
                  pl.BlockSpec((tm, tn), lambda j, i: (i, j))],
        out_specs=pl.BlockSpec((tm, tn), lambda j, i: (i, j)),
        out_shape=jax.ShapeDtypeStruct((m, n), F32),
        scratch_shapes=[pltpu.VMEM((tk, tn), BF16)],
        compiler_params=_cparams(("arbitrary", "arbitrary")),
        name=name,
    )(a, w, resid)


def _merge_kernel(oa_ref, ob_ref, ga_ref, gb_ref, bg_ref, wa_ref, wb_ref, o_ref):
    pa = jnp.dot(oa_ref[...], wa_ref[...].astype(BF16), preferred_element_type=F32)
    pb = jnp.dot(ob_ref[...], wb_ref[...].astype(BF16), preferred_element_type=F32)
    gate_a = jax.nn.sigmoid(ga_ref[...].astype(F32) + bg_ref[0:1, :])
    gate_b = jax.nn.sigmoid(gb_ref[...].astype(F32) + bg_ref[1:2, :])
    o_ref[...] = (gate_a * pa + gate_b * pb).astype(o_ref.dtype)


def _merge(o_a, o_b, gates, b_gate, w_proj_a, w_proj_b, *, tm, tn, name):
    m, ka = o_a.shape
    kb = o_b.shape[1]
    n = w_proj_a.shape[1]
    gb_blk = n // tn
    return pl.pallas_call(
        _merge_kernel,
        grid=(m // tm, n // tn),
        in_specs=[_row_spec(tm, ka),
                  _row_spec(tm, kb),
                  pl.BlockSpec((tm, tn), lambda i, j: (i, j)),
                  pl.BlockSpec((tm, tn), lambda i, j: (i, gb_blk + j)),
                  pl.BlockSpec((2, tn), lambda i, j: (0, j)),
                  pl.BlockSpec((ka, tn), lambda i, j: (0, j)),
                  pl.BlockSpec((kb, tn), lambda i, j: (0, j))],
        out_specs=pl.BlockSpec((tm, tn), lambda i, j: (i, j)),
        out_shape=jax.ShapeDtypeStruct((m, n), BF16),
        compiler_params=_cparams(("arbitrary", "arbitrary")),
        name=name,
    )(o_a, o_b, gates, gates, b_gate, w_proj_a, w_proj_b)


def _swiglu_kernel(h_ref, wg_ref, wu_ref, o_ref):
    h = h_ref[...]
    g = jnp.dot(h, wg_ref[...].astype(BF16), preferred_element_type=F32)
    u = jnp.dot(h, wu_ref[...].astype(BF16), preferred_element_type=F32)
    o_ref[...] = (g * jax.nn.sigmoid(g) * u).astype(o_ref.dtype)


def _swiglu(h, w_gate, w_up, *, tm, tn, name):
    m, k = h.shape
    n = w_gate.shape[1]
    return pl.pallas_call(
        _swiglu_kernel,
        grid=(m // tm, n // tn),
        in_specs=[_row_spec(tm, k, single_buffered=True),
                  pl.BlockSpec((k, tn), lambda i, j: (0, j)),
                  pl.BlockSpec((k, tn), lambda i, j: (0, j))],
        out_specs=pl.BlockSpec((tm, tn), lambda i, j: (i, j)),
        out_shape=jax.ShapeDtypeStruct((m, n), BF16),
        compiler_params=_cparams(("arbitrary", "arbitrary")),
        name=name,
    )(h, w_gate, w_up)


def _head_rms(x, g):
    ms = jnp.mean(x * x, axis=-1, keepdims=True)
    return x * lax.rsqrt(ms + EPS) * g


def _rope(x, cos, sin_signed):
    lane = lax.broadcasted_iota(jnp.int32, x.shape, 1)
    first_half = (lane % AXIS_ROPE_DIM) < (AXIS_ROPE_DIM // 2)
    partner = jnp.where(first_half,
                        pltpu.roll(x, HEAD_DIM - AXIS_ROPE_DIM // 2, 1),
                        pltpu.roll(x, AXIS_ROPE_DIM // 2, 1))
    return x * cos + partner * sin_signed


def _rope_tables(seq):
    rows = seq // GRID_W
    row = jnp.broadcast_to(jnp.arange(rows)[:, None], (rows, GRID_W)).reshape(-1).astype(F32)
    col = jnp.broadcast_to(jnp.arange(GRID_W)[None, :], (rows, GRID_W)).reshape(-1).astype(F32)
    inv = ROPE_THETA ** (-jnp.arange(0, AXIS_ROPE_DIM, 2, dtype=F32) / AXIS_ROPE_DIM)
    ang_row, ang_col = row[:, None] * inv, col[:, None] * inv
    cr, sr, cc, sc = jnp.cos(ang_row), jnp.sin(ang_row), jnp.cos(ang_col), jnp.sin(ang_col)
    cos = jnp.concatenate([cr, cr, cc, cc], axis=-1)
    sin_signed = jnp.concatenate([-sr, sr, -sc, sc], axis=-1)
    return cos, sin_signed


SOFTMAX_ROWS = 16
LOG2E = math.log2(math.e)


def _attn_a_kernel(q_ref, k_ref, v_ref, cq_ref, sq_ref, ck_ref, sk_ref, gq_ref, gk_ref,
                   o_ref, kn_ref, v1_ref, s_ref, p_ref):
    @pl.when(pl.program_id(2) == 0)
    def _():
        k = _head_rms(k_ref[0].astype(F32), gk_ref[...])
        kn_ref[...] = _rope(k, ck_ref[...], sk_ref[...]).astype(BF16)
        v1_ref[:, :HEAD_DIM] = v_ref[0]
        v1_ref[:, HEAD_DIM:] = jnp.ones((v_ref.shape[1], HEAD_DIM), BF16)

    kn = kn_ref[...]
    v1 = v1_ref[...]
    cq, sq, gq = cq_ref[...], sq_ref[...], gq_ref[...]
    tq = q_ref.shape[1]
    for g in range(N_A_GROUP):
        cols = slice(g * HEAD_DIM, (g + 1) * HEAD_DIM)
        slot = g % 2
        q = _head_rms(q_ref[0, :, cols].astype(F32), gq)
        q = (_rope(q, cq, sq) * (ATTN_SCALE * LOG2E)).astype(BF16)
        s_ref[slot] = lax.dot_general(q, kn, (((1,), (1,)), ((), ())), preferred_element_type=F32)
        for r in range(tq // SOFTMAX_ROWS):
            rows = pl.ds(r * SOFTMAX_ROWS, SOFTMAX_ROWS)
            s = s_ref[slot, rows, :]
            p_ref[slot, rows, :] = jnp.exp2(s - jnp.max(s, axis=-1, keepdims=True)).astype(BF16)
        ov = jnp.dot(p_ref[slot], v1, preferred_element_type=F32)
        o_ref[0, :, cols] = (ov[:, :HEAD_DIM] / ov[:, HEAD_DIM:]).astype(o_ref.dtype)


def _attention_a(qkv3, cos, sin_signed, gq, gk, *, tq):
    b, s, _ = qkv3.shape
    kb = N_A_HEADS
    vb = N_A_HEADS + N_A_KV_HEADS
    return pl.pallas_call(
        _attn_a_kernel,
        grid=(b, N_A_KV_HEADS, s // tq),
        in_specs=[pl.BlockSpec((1, tq, HEAD_BLOCK), lambda bi, kv, qi: (bi, qi, kv)),
                  pl.BlockSpec((1, s, HEAD_DIM), lambda bi, kv, qi: (bi, 0, kb + kv)),
                  pl.BlockSpec((1, s, HEAD_DIM), lambda bi, kv, qi: (bi, 0, vb + kv)),
                  pl.BlockSpec((tq, HEAD_DIM), lambda bi, kv, qi: (qi, 0)),
                  pl.BlockSpec((tq, HEAD_DIM), lambda bi, kv, qi: (qi, 0)),
                  pl.BlockSpec((s, HEAD_DIM), lambda bi, kv, qi: (0, 0)),
                  pl.BlockSpec((s, HEAD_DIM), lambda bi, kv, qi: (0, 0)),
                  pl.BlockSpec((1, HEAD_DIM), lambda bi, kv, qi: (0, 0)),
                  pl.BlockSpec((1, HEAD_DIM), lambda bi, kv, qi: (0, 0))],
        out_specs=pl.BlockSpec((1, tq, HEAD_BLOCK), lambda bi, kv, qi: (bi, qi, kv)),
        out_shape=jax.ShapeDtypeStruct((b, s, N_A_HEADS * HEAD_DIM), BF16),
        scratch_shapes=[pltpu.VMEM((s, HEAD_DIM), BF16),
                        pltpu.VMEM((s, 2 * HEAD_DIM), BF16),
                        pltpu.VMEM((2, tq, s), F32),
                        pltpu.VMEM((2, tq, s), BF16)],
        compiler_params=_cparams(("arbitrary", "arbitrary", "arbitrary")),
        name="attention_a",
    )(qkv3, qkv3, qkv3, cos, sin_signed, cos, sin_signed, gq, gk)


RADIUS = 64
assert all(w // (2 * d) == RADIUS for w, d in B_PATTERNS)


def _t5_bucket_np(rel):
    nb = REL_BUCKETS // 2
    max_exact = nb // 2
    side = np.where(rel > 0, nb, 0)
    n = np.abs(rel)
    nf = np.maximum(n, 1).astype(np.float32)
    large = max_exact + (np.log(nf / np.float32(max_exact))
                         / np.float32(math.log(REL_MAX_DIST / max_exact))
                         * np.float32(nb - max_exact)).astype(np.int32)
    large = np.minimum(large, nb - 1)
    return (side + np.where(n < max_exact, n, large)).astype(np.int32)


def _band_geometry(seq, dil):
    length = seq // dil
    kw = min(2 * Q_TILE, length)
    nblk = length // Q_TILE
    starts = [min(max(mb * Q_TILE - RADIUS, 0), length - kw) for mb in range(nblk)]
    shifts = sorted({st - mb * Q_TILE for mb, st in enumerate(starts)}, reverse=True)
    return length, kw, nblk, shifts


def _bucket_tiles(seq, dil):
    _, kw, _, shifts = _band_geometry(seq, dil)
    i = np.arange(Q_TILE)[:, None]
    j = np.arange(kw)[None, :]
    tiles = []
    for shift in shifts:
        rel = j - i + shift
        tiles.append(np.where(np.abs(rel) <= RADIUS, _t5_bucket_np(rel * dil), -1))
    return np.stack(tiles, axis=0).astype(np.int32)


def _attn_b_kernel(rb_ref, bk0_ref, bk1_ref, bk2_ref,
                   q0_ref, q1_ref, q2_ref, k0_ref, k1_ref, k2_ref, v0_ref, v1_ref, v2_ref,
                   gq_ref, gk_ref, o_ref,
                   qc_ref, kc_ref, vc_ref, on_ref, ln_ref, t0_ref, t1_ref, t2_ref, *, seq):
    head = pl.program_id(0)
    gq, gk = gq_ref[...], gk_ref[...]
    bk_refs = (bk0_ref, bk1_ref, bk2_ref)
    tile_refs = (t0_ref, t1_ref, t2_ref)
    q_refs, k_refs, v_refs = (q0_ref, q1_ref, q2_ref), (k0_ref, k1_ref, k2_ref), (v0_ref, v1_ref, v2_ref)

    @pl.when(pl.program_id(1) == 0)
    def _():
        for g in range(N_B_GROUPS):
            col = g * N_B_HEADS_PER_GROUP + head
            for var in range(bk_refs[g].shape[0]):
                bkt = bk_refs[g][var]

                def pick(b, acc, bkt=bkt, col=col):
                    return jnp.where(bkt == b, rb_ref[b, col], acc)

                tile_refs[g][var] = lax.fori_loop(0, REL_BUCKETS, pick,
                                                  jnp.full(bkt.shape, NEG_INF, F32))

    for g, (_, dil) in enumerate(B_PATTERNS):
        length, kw, nblk, shifts = _band_geometry(seq, dil)

        for c in range(dil):
            rows = pl.ds(c, length, stride=dil) if dil > 1 else pl.ds(0, length)
            dst = pl.ds(c * length, length)
            qc_ref[g, dst, :] = (_head_rms(q_refs[g][0, rows, :], gq) * ATTN_SCALE).astype(BF16)
            kc_ref[g, dst, :] = _head_rms(k_refs[g][0, rows, :], gk).astype(BF16)
            vc_ref[g, dst, :] = v_refs[g][0, rows, :].astype(BF16)

        for c in range(dil):
            for mb in range(nblk):
                start = min(max(mb * Q_TILE - RADIUS, 0), length - kw)
                var = shifts.index(start - mb * Q_TILE)
                q = qc_ref[g, pl.ds(c * length + mb * Q_TILE, Q_TILE), :]
                k = kc_ref[g, pl.ds(c * length + start, kw), :]
                v = vc_ref[g, pl.ds(c * length + start, kw), :]
                logits = lax.dot_general(q, k, (((1,), (1,)), ((), ())), preferred_element_type=F32)
                logits = logits + tile_refs[g][var]
                m = jnp.max(logits, axis=-1, keepdims=True)
                p = jnp.exp(logits - m)
                denom = jnp.sum(p, axis=-1, keepdims=True)
                o = jnp.dot(p.astype(BF16), v, preferred_element_type=F32) / denom
                t0 = c + dil * mb * Q_TILE
                rows = pl.ds(t0, Q_TILE, stride=dil) if dil > 1 else pl.ds(t0, Q_TILE)
                on_ref[g, rows, :] = o
                ln_ref[g, rows, :] = jnp.broadcast_to(m + jnp.log(denom), (Q_TILE, HEAD_DIM))

    chunk = 2 * Q_TILE

    def combine(i, carry):
        rows = pl.ds(pl.multiple_of(i * chunk, chunk), chunk)
        l0, l1, l2 = ln_ref[0, rows, :], ln_ref[1, rows, :], ln_ref[2, rows, :]
        mx = jnp.maximum(jnp.maximum(l0, l1), l2)
        e0, e1, e2 = jnp.exp(l0 - mx), jnp.exp(l1 - mx), jnp.exp(l2 - mx)
        num = e0 * on_ref[0, rows, :] + e1 * on_ref[1, rows, :] + e2 * on_ref[2, rows, :]
        o_ref[0, rows, :] = (num / (e0 + e1 + e2)).astype(o_ref.dtype)
        return carry

    lax.fori_loop(0, seq // chunk, combine, 0)


def _attention_b(qkv3, rel_bias, gq, gk):
    b, s, _ = qkv3.shape
    nh = N_B_HEADS_PER_GROUP
    bks = [jnp.asarray(_bucket_tiles(s, dil)) for _, dil in B_PATTERNS]

    def head_spec(part, g):
        base = (part * N_B_GROUPS + g) * nh
        return pl.BlockSpec((1, s, HEAD_DIM), lambda h, bi: (bi, 0, base + h))

    def full_spec(arr):
        nd = arr.ndim
        return pl.BlockSpec(arr.shape, lambda h, bi: (0,) * nd)

    in_specs = [pl.BlockSpec(memory_space=pltpu.SMEM)]
    in_specs += [full_spec(bk) for bk in bks]
    in_specs += [head_spec(part, g) for part in range(3) for g in range(N_B_GROUPS)]
    in_specs += [pl.BlockSpec((1, HEAD_DIM), lambda h, bi: (0, 0))] * 2
    scratch = [pltpu.VMEM((N_B_GROUPS, s, HEAD_DIM), BF16)] * 3
    scratch += [pltpu.VMEM((N_B_GROUPS, s, HEAD_DIM), F32)] * 2
    scratch += [pltpu.VMEM(bk.shape, F32) for bk in bks]
    return pl.pallas_call(
        functools.partial(_attn_b_kernel, seq=s),
        grid=(nh, b),
        in_specs=in_specs,
        out_specs=pl.BlockSpec((1, s, HEAD_DIM), lambda h, bi: (bi, 0, h)),
        out_shape=jax.ShapeDtypeStruct((b, s, nh * HEAD_DIM), BF16),
        scratch_shapes=scratch,
        compiler_params=_cparams(("arbitrary", "arbitrary")),
        name="attention_b",
    )(rel_bias, *bks, *([qkv3] * 9), gq, gk)


def kernel(x, norm1_g, w_in, b_gate, q_norm_a, k_norm_a, q_norm_b, k_norm_b, rel_bias,
           w_proj_a, w_proj_b, w_out, norm2_g, w_ffn_gate, w_ffn_up, w_ffn_down):
    b, s, d = x.shape
    m = b * s
    a_w = (N_A_HEADS + 2 * N_A_KV_HEADS) * HEAD_DIM
    b_w = 3 * N_B_GROUPS * N_B_HEADS_PER_GROUP * HEAD_DIM
    cos, sin_signed = _rope_tables(s)
    d_ff = w_ffn_gate.shape[-1]
    tk_down = d_ff // 2

    xf = x.reshape(m, d)
    for l in range(norm1_g.shape[0]):
        h = _rmsnorm(xf, norm1_g[l], name="rmsnorm1")
        qkv_a = _matmul(h, w_in[l], col0=0, ncols=a_w, tm=2048, tn=256, out_dtype=BF16,
                        name="in_proj_a")
        qkv_b = _matmul(h, w_in[l], col0=a_w, ncols=b_w, tm=2048, tn=256, out_dtype=F32,
                        name="in_proj_b")
        gates = _matmul(h, w_in[l], col0=a_w + b_w, ncols=2 * d, tm=2048, tn=256, out_dtype=BF16,
                        name="in_proj_gates")

        o_a = _attention_a(qkv_a.reshape(b, s, a_w), cos, sin_signed, q_norm_a[l].reshape(1, -1),
                           k_norm_a[l].reshape(1, -1), tq=256)
        o_b = _attention_b(qkv_b.reshape(b, s, b_w), rel_bias, q_norm_b[l].reshape(1, -1),
                           k_norm_b[l].reshape(1, -1))

        merged = _merge(o_a.reshape(m, -1), o_b.reshape(m, -1), gates, b_gate[l], w_proj_a[l],
                        w_proj_b[l], tm=2048, tn=256, name="gated_merge")
        xf = _matmul_resid_rows(merged, w_out[l], xf, tm=2048, tn=256, name="out_proj")

        h2 = _rmsnorm(xf, norm2_g[l], name="rmsnorm2")
        act = _swiglu(h2, w_ffn_gate[l], w_ffn_up[l], tm=2048, tn=256, name="swiglu_up")
        for kblock in range(d_ff // tk_down):
            xf = _matmul_resid(act, w_ffn_down[l], xf, tm=512, tn=512, tk=tk_down, kblock=kblock,
                               name=f"ffn_down{kblock}")
    return xf.reshape(b, s, d)
```

```python
import functools
import math

import jax
import jax.numpy as jnp
import numpy as np
from jax import lax
from jax.experimental import pallas as pl
from jax.experimental.pallas import tpu as pltpu

F32 = jnp.float32
BF16 = jnp.bfloat16

HEAD_DIM = 128
N_A_HEADS = 16
N_A_KV_HEADS = 4
N_A_GROUP = N_A_HEADS // N_A_KV_HEADS
B_PATTERNS = ((128, 1), (512, 4), (2048, 16))
N_B_GROUPS = len(B_PATTERNS)
N_B_HEADS_PER_GROUP = 4
GRID_W = 64
ROPE_THETA = 10000.0
AXIS_ROPE_DIM = HEAD_DIM // 2
REL_BUCKETS = 32
REL_MAX_DIST = 1024
EPS = 1e-6
NEG_INF = -1e30
ATTN_SCALE = HEAD_DIM ** -0.5

Q_TILE = 128
HEAD_BLOCK = N_A_GROUP * HEAD_DIM
VMEM_LIMIT = 56 * 1024 * 1024


def _cparams(sem):
    return pltpu.CompilerParams(dimension_semantics=sem, vmem_limit_bytes=VMEM_LIMIT)


def _rmsnorm_kernel(x_ref, g_ref, o_ref):
    x = x_ref[...]
    ms = jnp.mean(x * x, axis=-1, keepdims=True)
    o_ref[...] = (x * lax.rsqrt(ms + EPS) * g_ref[...]).astype(o_ref.dtype)


def _rmsnorm(x, g, *, tm=256, name):
    m, d = x.shape
    return pl.pallas_call(
        _rmsnorm_kernel,
        grid=(m // tm,),
        in_specs=[pl.BlockSpec((tm, d), lambda i: (i, 0)),
                  pl.BlockSpec((1, d), lambda i: (0, 0))],
        out_specs=pl.BlockSpec((tm, d), lambda i: (i, 0)),
        out_shape=jax.ShapeDtypeStruct((m, d), BF16),
        compiler_params=_cparams(("parallel",)),
        name=name,
    )(x, g.reshape(1, d))


def _row_spec(tm, k, single_buffered=False):
    if single_buffered:
        return pl.BlockSpec((tm, k), lambda i, j: (i, 0), pipeline_mode=pl.Buffered(1))
    return pl.BlockSpec((tm, k), lambda i, j: (i, 0))


def _mm_kernel(a_ref, w_ref, o_ref):
    w = w_ref[...].astype(BF16)
    o_ref[...] = jnp.dot(a_ref[...], w, preferred_element_type=F32).astype(o_ref.dtype)


def _matmul(a, w, *, col0, ncols, tm, tn, out_dtype, name):
    m, k = a.shape
    jb = col0 // tn
    return pl.pallas_call(
        _mm_kernel,
        grid=(m // tm, ncols // tn),
        in_specs=[_row_spec(tm, k),
                  pl.BlockSpec((k, tn), lambda i, j: (0, jb + j))],
        out_specs=pl.BlockSpec((tm, tn), lambda i, j: (i, j)),
        out_shape=jax.ShapeDtypeStruct((m, ncols), out_dtype),
        compiler_params=_cparams(("arbitrary", "arbitrary")),
        name=name,
    )(a, w)


def _mm_resid_norm_kernel(a_ref, w_ref, r_ref, g_ref, o_ref, og_ref, rs_ref, ss_ref, *, width):
    j = pl.program_id(1)
    y = r_ref[...] + jnp.dot(a_ref[...], w_ref[...].astype(BF16), preferred_element_type=F32)
    o_ref[...] = y
    og_ref[...] = (y * g_ref[...]).astype(og_ref.dtype)
    sq = y * y
    part = sq[:, :HEAD_DIM]
    for c in range(1, sq.shape[1] // HEAD_DIM):
        part = part + sq[:, c * HEAD_DIM:(c + 1) * HEAD_DIM]

    @pl.when(j == 0)
    def _():
        ss_ref[...] = part

    @pl.when(j > 0)
    def _():
        ss_ref[...] += part

    @pl.when(j == pl.num_programs(1) - 1)
    def _():
        ms = jnp.sum(ss_ref[...], axis=-1, keepdims=True) * (1.0 / width)
        rs_ref[...] = jnp.broadcast_to(lax.rsqrt(ms + EPS), rs_ref.shape)


def _matmul_resid_norm(a, w, resid, gain, *, tm, tn, rs_width, name):
    m, k = a.shape
    n = w.shape[1]
    return pl.pallas_call(
        functools.partial(_mm_resid_norm_kernel, width=n),
        grid=(m // tm, n // tn),
        in_specs=[_row_spec(tm, k, single_buffered=True),
                  pl.BlockSpec((k, tn), lambda i, j: (0, j)),
                  pl.BlockSpec((tm, tn), lambda i, j: (i, j)),
                  pl.BlockSpec((1, tn), lambda i, j: (0, j))],
        out_specs=[pl.BlockSpec((tm, tn), lambda i, j: (i, j)),
                   pl.BlockSpec((tm, tn), lambda i, j: (i, j)),
                   pl.BlockSpec((tm, rs_width), lambda i, j: (i, 0))],
        out_shape=[jax.ShapeDtypeStruct((m, n), F32),
                   jax.ShapeDtypeStruct((m, n), BF16),
                   jax.ShapeDtypeStruct((m, rs_width), F32)],
        scratch_shapes=[pltpu.VMEM((tm, HEAD_DIM), F32)],
        compiler_params=_cparams(("arbitrary", "arbitrary")),
        name=name,
    )(a, w, resid, gain.reshape(1, n))


def _mm_resid_kernel(a_ref, w_ref, r_ref, o_ref, wb_ref):
    @pl.when(pl.program_id(1) == 0)
    def _():
        wb_ref[...] = w_ref[...].astype(BF16)

    o_ref[...] = r_ref[...] + jnp.dot(a_ref[...], wb_ref[...], preferred_element_type=F32)


def _matmul_resid(a, w, resid, *, tm, tn, tk, kblock, name):
    m = a.shape[0]
    n = w.shape[1]
    return pl.pallas_call(
        _mm_resid_kernel,
        grid=(n // tn, m // tm),
        in_specs=[pl.BlockSpec((tm, tk), lambda j, i: (i, kblock)),
                  pl.BlockSpec((tk, tn), lambda j, i: (kblock, j)),
                  pl.BlockSpec((tm, tn), lambda j, i: (i, j))],
        out_specs=pl.BlockSpec((tm, tn), lambda j, i: (i, j)),
        out_shape=jax.ShapeDtypeStruct((m, n), F32),
        scratch_shapes=[pltpu.VMEM((tk, tn), BF16)],
        compiler_params=_cparams(("arbitrary", "arbitrary")),
        name=name,
    )(a, w, resid)


def _merge_kernel(oa_ref, ob_ref, ga_ref, gb_ref, bg_ref, wa_ref, wb_ref, o_ref):
    pa = jnp.dot(oa_ref[...], wa_ref[...].astype(BF16), preferred_element_type=F32)
    pb = jnp.dot(ob_ref[...], wb_ref[...].astype(BF16), preferred_element_type=F32)
    gate_a = jax.nn.sigmoid(ga_ref[...].astype(F32) + bg_ref[0:1, :])
    gate_b = jax.nn.sigmoid(gb_ref[...].astype(F32) + bg_ref[1:2, :])
    o_ref[...] = (gate_a * pa + gate_b * pb).astype(o_ref.dtype)


def _merge(o_a, o_b, gates, b_gate, w_proj_a, w_proj_b, *, tm, tn, name):
    m, ka = o_a.shape
    kb = o_b.shape[1]
    n = w_proj_a.shape[1]
    gb_blk = n // tn
    return pl.pallas_call(
        _merge_kernel,
        grid=(m // tm, n // tn),
        in_specs=[_row_spec(tm, ka),
                  _row_spec(tm, kb),
                  pl.BlockSpec((tm, tn), lambda i, j: (i, j)),
                  pl.BlockSpec((tm, tn), lambda i, j: (i, gb_blk + j)),
                  pl.BlockSpec((2, tn), lambda i, j: (0, j)),
                  pl.BlockSpec((ka, tn), lambda i, j: (0, j)),
                  pl.BlockSpec((kb, tn), lambda i, j: (0, j))],
        out_specs=pl.BlockSpec((tm, tn), lambda i, j: (i, j)),
        out_shape=jax.ShapeDtypeStruct((m, n), BF16),
        compiler_params=_cparams(("arbitrary", "arbitrary")),
        name=name,
    )(o_a, o_b, gates, gates, b_gate, w_proj_a, w_proj_b)


def _swiglu_kernel(h_ref, rs_ref, wg_ref, wu_ref, o_ref):
    h = h_ref[...]
    rs = rs_ref[...]
    g = jnp.dot(h, wg_ref[...].astype(BF16), preferred_element_type=F32) * rs
    u = jnp.dot(h, wu_ref[...].astype(BF16), preferred_element_type=F32) * rs
    o_ref[...] = (g * jax.nn.sigmoid(g) * u).astype(o_ref.dtype)


def _swiglu(h, row_scale, w_gate, w_up, *, tm, tn, name):
    m, k = h.shape
    n = w_gate.shape[1]
    return pl.pallas_call(
        _swiglu_kernel,
        grid=(m // tm, n // tn),
        in_specs=[_row_spec(tm, k, single_buffered=True),
                  pl.BlockSpec((tm, tn), lambda i, j: (i, 0)),
                  pl.BlockSpec((k, tn), lambda i, j: (0, j)),
                  pl.BlockSpec((k, tn), lambda i, j: (0, j))],
        out_specs=pl.BlockSpec((tm, tn), lambda i, j: (i, j)),
        out_shape=jax.ShapeDtypeStruct((m, n), BF16),
        compiler_params=_cparams(("arbitrary", "arbitrary")),
        name=name,
    )(h, row_scale, w_gate, w_up)


def _head_rms(x, g):
    ms = jnp.mean(x * x, axis=-1, keepdims=True)
    return x * lax.rsqrt(ms + EPS) * g


def _rope(x, cos, sin_signed):
    lane = lax.broadcasted_iota(jnp.int32, x.shape, 1)
    first_half = (lane % AXIS_ROPE_DIM) < (AXIS_ROPE_DIM // 2)
    partner = jnp.where(first_half,
                        pltpu.roll(x, HEAD_DIM - AXIS_ROPE_DIM // 2, 1),
                        pltpu.roll(x, AXIS_ROPE_DIM // 2, 1))
    return x * cos + partner * sin_signed


def _rope_tables(seq):
    rows = seq // GRID_W
    row = jnp.broadcast_to(jnp.arange(rows)[:, None], (rows, GRID_W)).reshape(-1).astype(F32)
    col = jnp.broadcast_to(jnp.arange(GRID_W)[None, :], (rows, GRID_W)).reshape(-1).astype(F32)
    inv = ROPE_THETA ** (-jnp.arange(0, AXIS_ROPE_DIM, 2, dtype=F32) / AXIS_ROPE_DIM)
    ang_row, ang_col = row[:, None] * inv, col[:, None] * inv
    cr, sr, cc, sc = jnp.cos(ang_row), jnp.sin(ang_row), jnp.cos(ang_col), jnp.sin(ang_col)
    cos = jnp.concatenate([cr, cr, cc, cc], axis=-1)
    sin_signed = jnp.concatenate([-sr, sr, -sc, sc], axis=-1)
    return cos, sin_signed


SOFTMAX_ROWS = 16
LOG2E = math.log2(math.e)


def _attn_a_kernel(q_ref, k_ref, v_ref, cq_ref, sq_ref, ck_ref, sk_ref, gq_ref, gk_ref,
                   o_ref, kn_ref, v1_ref, s_ref, p_ref):
    @pl.when(pl.program_id(2) == 0)
    def _():
        k = _head_rms(k_ref[0].astype(F32), gk_ref[...])
        kn_ref[...] = _rope(k, ck_ref[...], sk_ref[...]).astype(BF16)
        v1_ref[:, :HEAD_DIM] = v_ref[0]
        v1_ref[:, HEAD_DIM:] = jnp.ones((v_ref.shape[1], HEAD_DIM), BF16)

    kn = kn_ref[...]
    v1 = v1_ref[...]
    cq, sq, gq = cq_ref[...], sq_ref[...], gq_ref[...]
    tq = q_ref.shape[1]
    for g in range(N_A_GROUP):
        cols = slice(g * HEAD_DIM, (g + 1) * HEAD_DIM)
        slot = g % 2
        q = _head_rms(q_ref[0, :, cols].astype(F32), gq)
        q = (_rope(q, cq, sq) * (ATTN_SCALE * LOG2E)).astype(BF16)
        s_ref[slot] = lax.dot_general(q, kn, (((1,), (1,)), ((), ())), preferred_element_type=F32)
        for r in range(tq // SOFTMAX_ROWS):
            rows = pl.ds(r * SOFTMAX_ROWS, SOFTMAX_ROWS)
            s = s_ref[slot, rows, :]
            p_ref[slot, rows, :] = jnp.exp2(s - jnp.max(s, axis=-1, keepdims=True)).astype(BF16)
        ov = jnp.dot(p_ref[slot], v1, preferred_element_type=F32)
        o_ref[0, :, cols] = (ov[:, :HEAD_DIM] / ov[:, HEAD_DIM:]).astype(o_ref.dtype)


def _attention_a(qkv3, cos, sin_signed, gq, gk, *, tq):
    b, s, _ = qkv3.shape
    kb = N_A_HEADS
    vb = N_A_HEADS + N_A_KV_HEADS
    return pl.pallas_call(
        _attn_a_kernel,
        grid=(b, N_A_KV_HEADS, s // tq),
        in_specs=[pl.BlockSpec((1, tq, HEAD_BLOCK), lambda bi, kv, qi: (bi, qi, kv)),
                  pl.BlockSpec((1, s, HEAD_DIM), lambda bi, kv, qi: (bi, 0, kb + kv)),
                  pl.BlockSpec((1, s, HEAD_DIM), lambda bi, kv, qi: (bi, 0, vb + kv)),
                  pl.BlockSpec((tq, HEAD_DIM), lambda bi, kv, qi: (qi, 0)),
                  pl.BlockSpec((tq, HEAD_DIM), lambda bi, kv, qi: (qi, 0)),
                  pl.BlockSpec((s, HEAD_DIM), lambda bi, kv, qi: (0, 0)),
                  pl.BlockSpec((s, HEAD_DIM), lambda bi, kv, qi: (0, 0)),
                  pl.BlockSpec((1, HEAD_DIM), lambda bi, kv, qi: (0, 0)),
                  pl.BlockSpec((1, HEAD_DIM), lambda bi, kv, qi: (0, 0))],
        out_specs=pl.BlockSpec((1, tq, HEAD_BLOCK), lambda bi, kv, qi: (bi, qi, kv)),
        out_shape=jax.ShapeDtypeStruct((b, s, N_A_HEADS * HEAD_DIM), BF16),
        scratch_shapes=[pltpu.VMEM((s, HEAD_DIM), BF16),
                        pltpu.VMEM((s, 2 * HEAD_DIM), BF16),
                        pltpu.VMEM((2, tq, s), F32),
                        pltpu.VMEM((2, tq, s), BF16)],
        compiler_params=_cparams(("arbitrary", "arbitrary", "arbitrary")),
        name="attention_a",
    )(qkv3, qkv3, qkv3, cos, sin_signed, cos, sin_signed, gq, gk)


RADIUS = 64
assert all(w // (2 * d) == RADIUS for w, d in B_PATTERNS)


def _t5_bucket_np(rel):
    nb = REL_BUCKETS // 2
    max_exact = nb // 2
    side = np.where(rel > 0, nb, 0)
    n = np.abs(rel)
    nf = np.maximum(n, 1).astype(np.float32)
    large = max_exact + (np.log(nf / np.float32(max_exact))
                         / np.float32(math.log(REL_MAX_DIST / max_exact))
                         * np.float32(nb - max_exact)).astype(np.int32)
    large = np.minimum(large, nb - 1)
    return (side + np.where(n < max_exact, n, large)).astype(np.int32)


def _band_geometry(seq, dil):
    length = seq // dil
    kw = min(2 * Q_TILE, length)
    nblk = length // Q_TILE
    starts = [min(max(mb * Q_TILE - RADIUS, 0), length - kw) for mb in range(nblk)]
    shifts = sorted({st - mb * Q_TILE for mb, st in enumerate(starts)}, reverse=True)
    return length, kw, nblk, shifts


def _bucket_tiles(seq, dil):
    _, kw, _, shifts = _band_geometry(seq, dil)
    i = np.arange(Q_TILE)[:, None]
    j = np.arange(kw)[None, :]
    tiles = []
    for shift in shifts:
        rel = j - i + shift
        tiles.append(np.where(np.abs(rel) <= RADIUS, _t5_bucket_np(rel * dil), -1))
    return np.stack(tiles, axis=0).astype(np.int32)


def _attn_b_kernel(rb_ref, bk0_ref, bk1_ref, bk2_ref,
                   q0_ref, q1_ref, q2_ref, k0_ref, k1_ref, k2_ref, v0_ref, v1_ref, v2_ref,
                   gq_ref, gk_ref, o_ref,
                   qc_ref, kc_ref, vc_ref, on_ref, ln_ref, t0_ref, t1_ref, t2_ref, *, seq):
    head = pl.program_id(0)
    gq, gk = gq_ref[...], gk_ref[...]
    bk_refs = (bk0_ref, bk1_ref, bk2_ref)
    tile_refs = (t0_ref, t1_ref, t2_ref)
    q_refs, k_refs, v_refs = (q0_ref, q1_ref, q2_ref), (k0_ref, k1_ref, k2_ref), (v0_ref, v1_ref, v2_ref)

    @pl.when(pl.program_id(1) == 0)
    def _():
        for g in range(N_B_GROUPS):
            col = g * N_B_HEADS_PER_GROUP + head
            for var in range(bk_refs[g].shape[0]):
                bkt = bk_refs[g][var]

                def pick(b, acc, bkt=bkt, col=col):
                    return jnp.where(bkt == b, rb_ref[b, col], acc)

                tile_refs[g][var] = lax.fori_loop(0, REL_BUCKETS, pick,
                                                  jnp.full(bkt.shape, NEG_INF, F32))

    for g, (_, dil) in enumerate(B_PATTERNS):
        length, kw, nblk, shifts = _band_geometry(seq, dil)

        for c in range(dil):
            rows = pl.ds(c, length, stride=dil) if dil > 1 else pl.ds(0, length)
            dst = pl.ds(c * length, length)
            qc_ref[g, dst, :] = (_head_rms(q_refs[g][0, rows, :], gq) * ATTN_SCALE).astype(BF16)
            kc_ref[g, dst, :] = _head_rms(k_refs[g][0, rows, :], gk).astype(BF16)
            vc_ref[g, dst, :] = v_refs[g][0, rows, :].astype(BF16)

        for c in range(dil):
            for mb in range(nblk):
                start = min(max(mb * Q_TILE - RADIUS, 0), length - kw)
                var = shifts.index(start - mb * Q_TILE)
                q = qc_ref[g, pl.ds(c * length + mb * Q_TILE, Q_TILE), :]
                k = kc_ref[g, pl.ds(c * length + start, kw), :]
                v = vc_ref[g, pl.ds(c * length + start, kw), :]
                logits = lax.dot_general(q, k, (((1,), (1,)), ((), ())), preferred_element_type=F32)
                logits = logits + tile_refs[g][var]
                m = jnp.max(logits, axis=-1, keepdims=True)
                p = jnp.exp(logits - m)
                denom = jnp.sum(p, axis=-1, keepdims=True)
                o = jnp.dot(p.astype(BF16), v, preferred_element_type=F32) / denom
                t0 = c + dil * mb * Q_TILE
                rows = pl.ds(t0, Q_TILE, stride=dil) if dil > 1 else pl.ds(t0, Q_TILE)
                on_ref[g, rows, :] = o
                ln_ref[g, rows, :] = jnp.broadcast_to(m + jnp.log(denom), (Q_TILE, HEAD_DIM))

    chunk = 2 * Q_TILE

    def combine(i, carry):
        rows = pl.ds(pl.multiple_of(i * chunk, chunk), chunk)
        l0, l1, l2 = ln_ref[0, rows, :], ln_ref[1, rows, :], ln_ref[2, rows, :]
        mx = jnp.maximum(jnp.maximum(l0, l1), l2)
        e0, e1, e2 = jnp.exp(l0 - mx), jnp.exp(l1 - mx), jnp.exp(l2 - mx)
        num = e0 * on_ref[0, rows, :] + e1 * on_ref[1, rows, :] + e2 * on_ref[2, rows, :]
        o_ref[0, rows, :] = (num / (e0 + e1 + e2)).astype(o_ref.dtype)
        return carry

    lax.fori_loop(0, seq // chunk, combine, 0)


def _attention_b(qkv3, rel_bias, gq, gk):
    b, s, _ = qkv3.shape
    nh = N_B_HEADS_PER_GROUP
    bks = [jnp.asarray(_bucket_tiles(s, dil)) for _, dil in B_PATTERNS]

    def head_spec(part, g):
        base = (part * N_B_GROUPS + g) * nh
        return pl.BlockSpec((1, s, HEAD_DIM), lambda h, bi: (bi, 0, base + h))

    def full_spec(arr):
        nd = arr.ndim
        return pl.BlockSpec(arr.shape, lambda h, bi: (0,) * nd)

    in_specs = [pl.BlockSpec(memory_space=pltpu.SMEM)]
    in_specs += [full_spec(bk) for bk in bks]
    in_specs += [head_spec(part, g) for part in range(3) for g in range(N_B_GROUPS)]
    in_specs += [pl.BlockSpec((1, HEAD_DIM), lambda h, bi: (0, 0))] * 2
    scratch = [pltpu.VMEM((N_B_GROUPS, s, HEAD_DIM), BF16)] * 3
    scratch += [pltpu.VMEM((N_B_GROUPS, s, HEAD_DIM), F32)] * 2
    scratch += [pltpu.VMEM(bk.shape, F32) for bk in bks]
    return pl.pallas_call(
        functools.partial(_attn_b_kernel, seq=s),
        grid=(nh, b),
        in_specs=in_specs,
        out_specs=pl.BlockSpec((1, s, HEAD_DIM), lambda h, bi: (bi, 0, h)),
        out_shape=jax.ShapeDtypeStruct((b, s, nh * HEAD_DIM), BF16),
        scratch_shapes=scratch,
        compiler_params=_cparams(("arbitrary", "arbitrary")),
        name="attention_b",
    )(rel_bias, *bks, *([qkv3] * 9), gq, gk)


def kernel(x, norm1_g, w_in, b_gate, q_norm_a, k_norm_a, q_norm_b, k_norm_b, rel_bias,
           w_proj_a, w_proj_b, w_out, norm2_g, w_ffn_gate, w_ffn_up, w_ffn_down):
    b, s, d = x.shape
    m = b * s
    a_w = (N_A_HEADS + 2 * N_A_KV_HEADS) * HEAD_DIM
    b_w = 3 * N_B_GROUPS * N_B_HEADS_PER_GROUP * HEAD_DIM
    cos, sin_signed = _rope_tables(s)
    d_ff = w_ffn_gate.shape[-1]
    tk_down = d_ff // 2

    xf = x.reshape(m, d)
    for l in range(norm1_g.shape[0]):
        h = _rmsnorm(xf, norm1_g[l], name="rmsnorm1")
        qkv_a = _matmul(h, w_in[l], col0=0, ncols=a_w, tm=2048, tn=256, out_dtype=BF16,
                        name="in_proj_a")
        qkv_b = _matmul(h, w_in[l], col0=a_w, ncols=b_w, tm=2048, tn=256, out_dtype=F32,
                        name="in_proj_b")
        gates = _matmul(h, w_in[l], col0=a_w + b_w, ncols=2 * d, tm=2048, tn=256, out_dtype=BF16,
                        name="in_proj_gates")

        o_a = _attention_a(qkv_a.reshape(b, s, a_w), cos, sin_signed, q_norm_a[l].reshape(1, -1),
                           k_norm_a[l].reshape(1, -1), tq=256)
        o_b = _attention_b(qkv_b.reshape(b, s, b_w), rel_bias, q_norm_b[l].reshape(1, -1),
                           k_norm_b[l].reshape(1, -1))

        merged = _merge(o_a.reshape(m, -1), o_b.reshape(m, -1), gates, b_gate[l], w_proj_a[l],
                        w_proj_b[l], tm=2048, tn=512, name="gated_merge")
        tn_ffn = 256
        xf, xg, rs = _matmul_resid_norm(merged, w_out[l], xf, norm2_g[l], tm=2048, tn=256,
                                        rs_width=tn_ffn, name="out_proj")
        act = _swiglu(xg, rs, w_ffn_gate[l], w_ffn_up[l], tm=2048, tn=tn_ffn, name="swiglu_up")
        for kblock in range(d_ff // tk_down):
            xf = _matmul_resid(act, w_ffn_down[l], xf, tm=512, tn=512, tk=tk_down, kblock=kblock,
                               name=f"ffn_down{kblock}")
    return xf.reshape(b, s, d)
```

```python
import functools
import math

import jax
import jax.numpy as jnp
import numpy as np
from jax import lax
from jax.experimental import pallas as pl
from jax.experimental.pallas import tpu as pltpu

F32 = jnp.float32
BF16 = jnp.bfloat16

HEAD_DIM = 128
N_A_HEADS = 16
N_A_KV_HEADS = 4
N_A_GROUP = N_A_HEADS // N_A_KV_HEADS
B_PATTERNS = ((128, 1), (512, 4), (2048, 16))
N_B_GROUPS = len(B_PATTERNS)
N_B_HEADS_PER_GROUP = 4
GRID_W = 64
ROPE_THETA = 10000.0
AXIS_ROPE_DIM = HEAD_DIM // 2
REL_BUCKETS = 32
REL_MAX_DIST = 1024
EPS = 1e-6
NEG_INF = -1e30
ATTN_SCALE = HEAD_DIM ** -0.5

Q_TILE = 128
HEAD_BLOCK = N_A_GROUP * HEAD_DIM
VMEM_LIMIT = 62 * 1024 * 1024


def _cparams(sem, vmem_limit=VMEM_LIMIT):
    return pltpu.CompilerParams(dimension_semantics=sem, vmem_limit_bytes=vmem_limit)


def _rmsnorm_kernel(x_ref, g_ref, o_ref):
    x = x_ref[...]
    ms = jnp.mean(x * x, axis=-1, keepdims=True)
    o_ref[...] = (x * lax.rsqrt(ms + EPS) * g_ref[...]).astype(o_ref.dtype)


def _rmsnorm(x, g, *, tm=512, name):
    m, d = x.shape
    return pl.pallas_call(
        _rmsnorm_kernel,
        grid=(m // tm,),
        in_specs=[pl.BlockSpec((tm, d), lambda i: (i, 0)),
                  pl.BlockSpec((1, d), lambda i: (0, 0))],
        out_specs=pl.BlockSpec((tm, d), lambda i: (i, 0)),
        out_shape=jax.ShapeDtypeStruct((m, d), BF16),
        compiler_params=_cparams(("parallel",)),
        name=name,
    )(x, g.reshape(1, d))


ROW_CHUNKS = 2


def _row_spec(tm, k, single_buffered=False):
    if single_buffered:
        return pl.BlockSpec((tm, k), lambda i, j: (i, 0), pipeline_mode=pl.Buffered(1))
    return pl.BlockSpec((tm, k), lambda i, j: (i, 0))


def _mm_kernel(a_ref, w_ref, o_ref):
    w = w_ref[...].astype(BF16)
    o_ref[...] = jnp.dot(a_ref[...], w, preferred_element_type=F32).astype(o_ref.dtype)


def _matmul(a, w, *, col0, ncols, tm, tn, out_dtype, name):
    m, k = a.shape
    jb = col0 // tn
    return pl.pallas_call(
        _mm_kernel,
        grid=(m // tm, ncols // tn),
        in_specs=[_row_spec(tm, k),
                  pl.BlockSpec((k, tn), lambda i, j: (0, jb + j))],
        out_specs=pl.BlockSpec((tm, tn), lambda i, j: (i, j)),
        out_shape=jax.ShapeDtypeStruct((m, ncols), out_dtype),
        compiler_params=_cparams(("arbitrary", "arbitrary")),
        name=name,
    )(a, w)


def _mm_resid_norm_kernel(a_ref, w_ref, r_ref, g_ref, o_ref, og_ref, rs_ref, ss_ref, *, width):
    j = pl.program_id(1)
    w = w_ref[...].astype(BF16)
    g = g_ref[...]
    tm = a_ref.shape[0]
    chunk = tm // ROW_CHUNKS
    accs = [jnp.dot(a_ref[c * chunk:(c + 1) * chunk, :], w, preferred_element_type=F32)
            for c in range(ROW_CHUNKS)]
    for c, acc in enumerate(accs):
        rows = slice(c * chunk, (c + 1) * chunk)
        y = r_ref[rows, :] + acc
        o_ref[rows, :] = y
        og_ref[rows, :] = (y * g).astype(og_ref.dtype)
        sq = y * y
        part = sq[:, :HEAD_DIM]
        for cc in range(1, sq.shape[1] // HEAD_DIM):
            part = part + sq[:, cc * HEAD_DIM:(cc + 1) * HEAD_DIM]
        ss_ref[rows, :] = jnp.where(j == 0, part, ss_ref[rows, :] + part)

    @pl.when(j == pl.num_programs(1) - 1)
    def _():
        ms = jnp.sum(ss_ref[...], axis=-1, keepdims=True) * (1.0 / width)
        rs_ref[...] = jnp.broadcast_to(lax.rsqrt(ms + EPS), rs_ref.shape)


def _matmul_resid_norm(a, w, resid, gain, *, tm, tn, rs_width, name):
    m, k = a.shape
    n = w.shape[1]
    return pl.pallas_call(
        functools.partial(_mm_resid_norm_kernel, width=n),
        grid=(m // tm, n // tn),
        in_specs=[_row_spec(tm, k, single_buffered=True),
                  pl.BlockSpec((k, tn), lambda i, j: (0, j)),
                  pl.BlockSpec((tm, tn), lambda i, j: (i, j)),
                  pl.BlockSpec((1, tn), lambda i, j: (0, j))],
        out_specs=[pl.BlockSpec((tm, tn), lambda i, j: (i, j)),
                   pl.BlockSpec((tm, tn), lambda i, j: (i, j)),
                   pl.BlockSpec((tm, rs_width), lambda i, j: (i, 0))],
        out_shape=[jax.ShapeDtypeStruct((m, n), F32),
                   jax.ShapeDtypeStruct((m, n), BF16),
                   jax.ShapeDtypeStruct((m, rs_width), F32)],
        scratch_shapes=[pltpu.VMEM((tm, HEAD_DIM), F32)],
        compiler_params=_cparams(("arbitrary", "arbitrary")),
        name=name,
    )(a, w, resid, gain.reshape(1, n))


def _mm_resid_rows_kernel(a_ref, w_ref, r_ref, o_ref):
    w = w_ref[...].astype(BF16)
    o_ref[...] = r_ref[...] + jnp.dot(a_ref[...], w, preferred_element_type=F32)


def _matmul_resid_rows(a, w, resid, *, tm, tn, vmem_limit, name):
    m, k = a.shape
    n = w.shape[1]
    return pl.pallas_call(
        _mm_resid_rows_kernel,
        grid=(m // tm, n // tn),
        in_specs=[_row_spec(tm, k, single_buffered=True),
                  pl.BlockSpec((k, tn), lambda i, j: (0, j)),
                  pl.BlockSpec((tm, tn), lambda i, j: (i, j))],
        out_specs=pl.BlockSpec((tm, tn), lambda i, j: (i, j)),
        out_shape=jax.ShapeDtypeStruct((m, n), F32),
        compiler_params=_cparams(("arbitrary", "arbitrary"), vmem_limit),
        name=name,
    )(a, w, resid)


def _mm_resid_kernel(a_ref, w_ref, r_ref, o_ref, wb_ref):
    @pl.when(pl.program_id(1) == 0)
    def _():
        wb_ref[...] = w_ref[...].astype(BF16)

    o_ref[...] = r_ref[...] + jnp.dot(a_ref[...], wb_ref[...], preferred_element_type=F32)


def _matmul_resid(a, w, resid, *, tm, tn, tk, kblock, name):
    m = a.shape[0]
    n = w.shape[1]
    return pl.pallas_call(
        _mm_resid_kernel,
        grid=(n // tn, m // tm),
        in_specs=[pl.BlockSpec((tm, tk), lambda j, i: (i, kblock)),
                  pl.BlockSpec((tk, tn), lambda j, i: (kblock, j)),
                  pl.BlockSpec((tm, tn), lambda j, i: (i, j))],
        out_specs=pl.BlockSpec((tm, tn), lambda j, i: (i, j)),
        out_shape=jax.ShapeDtypeStruct((m, n), F32),
        scratch_shapes=[pltpu.VMEM((tk, tn), BF16)],
        compiler_params=_cparams(("arbitrary", "arbitrary")),
        name=name,
    )(a, w, resid)


def _merge_kernel(oa_ref, ob_ref, ga_ref, gb_ref, bg_ref, wa_ref, wb_ref, o_ref):
    wa = wa_ref[...].astype(BF16)
    wb = wb_ref[...].astype(BF16)
    chunk = oa_ref.shape[0] // ROW_CHUNKS
    prods = []
    for c in range(ROW_CHUNKS):
        rows = slice(c * chunk, (c + 1) * chunk)
        prods.append((jnp.dot(oa_ref[rows, :], wa, preferred_element_type=F32),
                      jnp.dot(ob_ref[rows, :], wb, preferred_element_type=F32)))
    for c, (pa, pb) in enumerate(prods):
        rows = slice(c * chunk, (c + 1) * chunk)
        gate_a = jax.nn.sigmoid(ga_ref[rows, :].astype(F32) + bg_ref[0:1, :])
        gate_b = jax.nn.sigmoid(gb_ref[rows, :].astype(F32) + bg_ref[1:2, :])
        o_ref[rows, :] = (gate_a * pa + gate_b * pb).astype(o_ref.dtype)


def _merge(o_a, o_b, gates, b_gate, w_proj_a, w_proj_b, *, tm, tn, name):
    m, ka = o_a.shape
    kb = o_b.shape[1]
    n = w_proj_a.shape[1]
    gb_blk = n // tn
    return pl.pallas_call(
        _merge_kernel,
        grid=(m // tm, n // tn),
        in_specs=[_row_spec(tm, ka),
                  _row_spec(tm, kb),
                  pl.BlockSpec((tm, tn), lambda i, j: (i, j)),
                  pl.BlockSpec((tm, tn), lambda i, j: (i, gb_blk + j)),
                  pl.BlockSpec((2, tn), lambda i, j: (0, j)),
                  pl.BlockSpec((ka, tn), lambda i, j: (0, j)),
                  pl.BlockSpec((kb, tn), lambda i, j: (0, j))],
        out_specs=pl.BlockSpec((tm, tn), lambda i, j: (i, j)),
        out_shape=jax.ShapeDtypeStruct((m, n), BF16),
        compiler_params=_cparams(("arbitrary", "arbitrary")),
        name=name,
    )(o_a, o_b, gates, gates, b_gate, w_proj_a, w_proj_b)


def _swiglu_kernel(h_ref, rs_ref, wg_ref, wu_ref, o_ref):
    wg = wg_ref[...].astype(BF16)
    wu = wu_ref[...].astype(BF16)
    chunk = h_ref.shape[0] // ROW_CHUNKS
    prods = []
    for c in range(ROW_CHUNKS):
        h = h_ref[c * chunk:(c + 1) * chunk, :]
        prods.append((jnp.dot(h, wg, preferred_element_type=F32),
                      jnp.dot(h, wu, preferred_element_type=F32)))
    for c, (g, u) in enumerate(prods):
        rows = slice(c * chunk, (c + 1) * chunk)
        rs = rs_ref[rows, :]
        rs = jnp.concatenate([rs] * (o_ref.shape[1] // rs.shape[1]), axis=1)
        g = g * rs
        o_ref[rows, :] = (g * jax.nn.sigmoid(g) * (u * rs)).astype(o_ref.dtype)


def _swiglu(h, row_scale, w_gate, w_up, *, tm, tn, name):
    m, k = h.shape
    n = w_gate.shape[1]
    return pl.pallas_call(
        _swiglu_kernel,
        grid=(m // tm, n // tn),
        in_specs=[_row_spec(tm, k),
                  pl.BlockSpec((tm, row_scale.shape[1]), lambda i, j: (i, 0)),
                  pl.BlockSpec((k, tn), lambda i, j: (0, j)),
                  pl.BlockSpec((k, tn), lambda i, j: (0, j))],
        out_specs=pl.BlockSpec((tm, tn), lambda i, j: (i, j)),
        out_shape=jax.ShapeDtypeStruct((m, n), BF16),
        compiler_params=_cparams(("arbitrary", "arbitrary")),
        name=name,
    )(h, row_scale, w_gate, w_up)


def _head_rms(x, g):
    ms = jnp.mean(x * x, axis=-1, keepdims=True)
    return x * lax.rsqrt(ms + EPS) * g


def _rope(x, cos, sin_signed):
    lane = lax.broadcasted_iota(jnp.int32, x.shape, 1)
    first_half = (lane % AXIS_ROPE_DIM) < (AXIS_ROPE_DIM // 2)
    partner = jnp.where(first_half,
                        pltpu.roll(x, HEAD_DIM - AXIS_ROPE_DIM // 2, 1),
                        pltpu.roll(x, AXIS_ROPE_DIM // 2, 1))
    return x * cos + partner * sin_signed


def _rope_tables(seq):
    rows = seq // GRID_W
    row = jnp.broadcast_to(jnp.arange(rows)[:, None], (rows, GRID_W)).reshape(-1).astype(F32)
    col = jnp.broadcast_to(jnp.arange(GRID_W)[None, :], (rows, GRID_W)).reshape(-1).astype(F32)
    inv = ROPE_THETA ** (-jnp.arange(0, AXIS_ROPE_DIM, 2, dtype=F32) / AXIS_ROPE_DIM)
    ang_row, ang_col = row[:, None] * inv, col[:, None] * inv
    cr, sr, cc, sc = jnp.cos(ang_row), jnp.sin(ang_row), jnp.cos(ang_col), jnp.sin(ang_col)
    cos = jnp.concatenate([cr, cr, cc, cc], axis=-1)
    sin_signed = jnp.concatenate([-sr, sr, -sc, sc], axis=-1)
    return cos, sin_signed


SOFTMAX_ROWS = 16
LOG2E = math.log2(math.e)


def _attn_a_kernel(q_ref, k_ref, v_ref, cq_ref, sq_ref, ck_ref, sk_ref, gq_ref, gk_ref,
                   o_ref, kn_ref, v1_ref, s_ref, p_ref):
    @pl.when(pl.program_id(2) == 0)
    def _():
        k = _head_rms(k_ref[0].astype(F32), gk_ref[...])
        kn_ref[...] = _rope(k, ck_ref[...], sk_ref[...]).astype(BF16)
        v1_ref[:, :HEAD_DIM] = v_ref[0]
        v1_ref[:, HEAD_DIM:] = jnp.ones((v_ref.shape[1], HEAD_DIM), BF16)

    kn = kn_ref[...]
    v1 = v1_ref[...]
    cq, sq, gq = cq_ref[...], sq_ref[...], gq_ref[...]
    tq = q_ref.shape[1]
    def scores(g):
        q = _head_rms(q_ref[0, :, g * HEAD_DIM:(g + 1) * HEAD_DIM].astype(F32), gq)
        q = (_rope(q, cq, sq) * (ATTN_SCALE * LOG2E)).astype(BF16)
        s_ref[g % 2] = lax.dot_general(q, kn, (((1,), (1,)), ((), ())),
                                       preferred_element_type=F32)

    def softmax(g):
        slot = g % 2
        for r in range(tq // SOFTMAX_ROWS):
            rows = pl.ds(r * SOFTMAX_ROWS, SOFTMAX_ROWS)
            s = s_ref[slot, rows, :]
            p_ref[slot, rows, :] = jnp.exp2(s - jnp.max(s, axis=-1, keepdims=True)).astype(BF16)

    def weighted_values(g):
        ov = jnp.dot(p_ref[g % 2], v1, preferred_element_type=F32)
        o_ref[0, :, g * HEAD_DIM:(g + 1) * HEAD_DIM] = (
            ov[:, :HEAD_DIM] / ov[:, HEAD_DIM:]).astype(o_ref.dtype)

    scores(0)
    for g in range(N_A_GROUP):
        if g + 1 < N_A_GROUP:
            scores(g + 1)
        softmax(g)
        weighted_values(g)


def _attention_a(qkv3, cos, sin_signed, gq, gk, *, tq):
    b, s, _ = qkv3.shape
    kb = N_A_HEADS
    vb = N_A_HEADS + N_A_KV_HEADS
    return pl.pallas_call(
        _attn_a_kernel,
        grid=(b, N_A_KV_HEADS, s // tq),
        in_specs=[pl.BlockSpec((1, tq, HEAD_BLOCK), lambda bi, kv, qi: (bi, qi, kv)),
                  pl.BlockSpec((1, s, HEAD_DIM), lambda bi, kv, qi: (bi, 0, kb + kv)),
                  pl.BlockSpec((1, s, HEAD_DIM), lambda bi, kv, qi: (bi, 0, vb + kv)),
                  pl.BlockSpec((tq, HEAD_DIM), lambda bi, kv, qi: (qi, 0)),
                  pl.BlockSpec((tq, HEAD_DIM), lambda bi, kv, qi: (qi, 0)),
                  pl.BlockSpec((s, HEAD_DIM), lambda bi, kv, qi: (0, 0)),
                  pl.BlockSpec((s, HEAD_DIM), lambda bi, kv, qi: (0, 0)),
                  pl.BlockSpec((1, HEAD_DIM), lambda bi, kv, qi: (0, 0)),
                  pl.BlockSpec((1, HEAD_DIM), lambda bi, kv, qi: (0, 0))],
        out_specs=pl.BlockSpec((1, tq, HEAD_BLOCK), lambda bi, kv, qi: (bi, qi, kv)),
        out_shape=jax.ShapeDtypeStruct((b, s, N_A_HEADS * HEAD_DIM), BF16),
        scratch_shapes=[pltpu.VMEM((s, HEAD_DIM), BF16),
                        pltpu.VMEM((s, 2 * HEAD_DIM), BF16),
                        pltpu.VMEM((2, tq, s), F32),
                        pltpu.VMEM((2, tq, s), BF16)],
        compiler_params=_cparams(("arbitrary", "arbitrary", "arbitrary")),
        name="attention_a",
    )(qkv3, qkv3, qkv3, cos, sin_signed, cos, sin_signed, gq, gk)


RADIUS = 64
assert all(w // (2 * d) == RADIUS for w, d in B_PATTERNS)


def _t5_bucket_np(rel):
    nb = REL_BUCKETS // 2
    max_exact = nb // 2
    side = np.where(rel > 0, nb, 0)
    n = np.abs(rel)
    nf = np.maximum(n, 1).astype(np.float32)
    large = max_exact + (np.log(nf / np.float32(max_exact))
                         / np.float32(math.log(REL_MAX_DIST / max_exact))
                         * np.float32(nb - max_exact)).astype(np.int32)
    large = np.minimum(large, nb - 1)
    return (side + np.where(n < max_exact, n, large)).astype(np.int32)


def _band_geometry(seq, dil):
    length = seq // dil
    kw = min(2 * Q_TILE, length)
    nblk = length // Q_TILE
    starts = [min(max(mb * Q_TILE - RADIUS, 0), length - kw) for mb in range(nblk)]
    shifts = sorted({st - mb * Q_TILE for mb, st in enumerate(starts)}, reverse=True)
    return length, kw, nblk, shifts


def _bucket_tiles(seq, dil):
    _, kw, _, shifts = _band_geometry(seq, dil)
    i = np.arange(Q_TILE)[:, None]
    j = np.arange(kw)[None, :]
    tiles = []
    for shift in shifts:
        rel = j - i + shift
        tiles.append(np.where(np.abs(rel) <= RADIUS, _t5_bucket_np(rel * dil), -1))
    return np.stack(tiles, axis=0).astype(np.int32)


def _attn_b_kernel(rb_ref, bk0_ref, bk1_ref, bk2_ref,
                   q0_ref, q1_ref, q2_ref, k0_ref, k1_ref, k2_ref, v0_ref, v1_ref, v2_ref,
                   gq_ref, gk_ref, o_ref,
                   qc_ref, kc_ref, vc_ref, on_ref, ln_ref, t0_ref, t1_ref, t2_ref, *, seq):
    head = pl.program_id(0)
    gq, gk = gq_ref[...], gk_ref[...]
    bk_refs = (bk0_ref, bk1_ref, bk2_ref)
    tile_refs = (t0_ref, t1_ref, t2_ref)
    q_refs, k_refs, v_refs = (q0_ref, q1_ref, q2_ref), (k0_ref, k1_ref, k2_ref), (v0_ref, v1_ref, v2_ref)

    @pl.when(pl.program_id(1) == 0)
    def _():
        for g in range(N_B_GROUPS):
            col = g * N_B_HEADS_PER_GROUP + head
            for var in range(bk_refs[g].shape[0]):
                bkt = bk_refs[g][var]

                def pick(b, acc, bkt=bkt, col=col):
                    return jnp.where(bkt == b, rb_ref[b, col], acc)

                tile_refs[g][var] = lax.fori_loop(0, REL_BUCKETS, pick,
                                                  jnp.full(bkt.shape, NEG_INF, F32))

    for g, (_, dil) in enumerate(B_PATTERNS):
        length, kw, nblk, shifts = _band_geometry(seq, dil)

        for c in range(dil):
            rows = pl.ds(c, length, stride=dil) if dil > 1 else pl.ds(0, length)
            dst = pl.ds(c * length, length)
            qc_ref[g, dst, :] = (_head_rms(q_refs[g][0, rows, :], gq) * ATTN_SCALE).astype(BF16)
            kc_ref[g, dst, :] = _head_rms(k_refs[g][0, rows, :], gk).astype(BF16)
            vc_ref[g, dst, :] = v_refs[g][0, rows, :].astype(BF16)

        for c in range(dil):
            for mb in range(nblk):
                start = min(max(mb * Q_TILE - RADIUS, 0), length - kw)
                var = shifts.index(start - mb * Q_TILE)
                q = qc_ref[g, pl.ds(c * length + mb * Q_TILE, Q_TILE), :]
                k = kc_ref[g, pl.ds(c * length + start, kw), :]
                v = vc_ref[g, pl.ds(c * length + start, kw), :]
                logits = lax.dot_general(q, k, (((1,), (1,)), ((), ())), preferred_element_type=F32)
                logits = logits + tile_refs[g][var]
                m = jnp.max(logits, axis=-1, keepdims=True)
                p = jnp.exp(logits - m)
                denom = jnp.sum(p, axis=-1, keepdims=True)
                o = jnp.dot(p.astype(BF16), v, preferred_element_type=F32) / denom
                t0 = c + dil * mb * Q_TILE
                rows = pl.ds(t0, Q_TILE, stride=dil) if dil > 1 else pl.ds(t0, Q_TILE)
                on_ref[g, rows, :] = o
                ln_ref[g, rows, :] = jnp.broadcast_to(m + jnp.log(denom), (Q_TILE, HEAD_DIM))

    chunk = 2 * Q_TILE

    def combine(i, carry):
        rows = pl.ds(pl.multiple_of(i * chunk, chunk), chunk)
        l0, l1, l2 = ln_ref[0, rows, :], ln_ref[1, rows, :], ln_ref[2, rows, :]
        mx = jnp.maximum(jnp.maximum(l0, l1), l2)
        e0, e1, e2 = jnp.exp(l0 - mx), jnp.exp(l1 - mx), jnp.exp(l2 - mx)
        num = e0 * on_ref[0, rows, :] + e1 * on_ref[1, rows, :] + e2 * on_ref[2, rows, :]
        o_ref[0, rows, :] = (num / (e0 + e1 + e2)).astype(o_ref.dtype)
        return carry

    lax.fori_loop(0, seq // chunk, combine, 0)


def _attention_b(qkv3, rel_bias, gq, gk):
    b, s, _ = qkv3.shape
    nh = N_B_HEADS_PER_GROUP
    bks = [jnp.asarray(_bucket_tiles(s, dil)) for _, dil in B_PATTERNS]

    def head_spec(part, g):
        base = (part * N_B_GROUPS + g) * nh
        return pl.BlockSpec((1, s, HEAD_DIM), lambda h, bi: (bi, 0, base + h))

    def full_spec(arr):
        nd = arr.ndim
        return pl.BlockSpec(arr.shape, lambda h, bi: (0,) * nd)

    in_specs = [pl.BlockSpec(memory_space=pltpu.SMEM)]
    in_specs += [full_spec(bk) for bk in bks]
    in_specs += [head_spec(part, g) for part in range(3) for g in range(N_B_GROUPS)]
    in_specs += [pl.BlockSpec((1, HEAD_DIM), lambda h, bi: (0, 0))] * 2
    scratch = [pltpu.VMEM((N_B_GROUPS, s, HEAD_DIM), BF16)] * 3
    scratch += [pltpu.VMEM((N_B_GROUPS, s, HEAD_DIM), F32)] * 2
    scratch += [pltpu.VMEM(bk.shape, F32) for bk in bks]
    return pl.pallas_call(
        functools.partial(_attn_b_kernel, seq=s),
        grid=(nh, b),
        in_specs=in_specs,
        out_specs=pl.BlockSpec((1, s, HEAD_DIM), lambda h, bi: (bi, 0, h)),
        out_shape=jax.ShapeDtypeStruct((b, s, nh * HEAD_DIM), BF16),
        scratch_shapes=scratch,
        compiler_params=_cparams(("arbitrary", "arbitrary")),
        name="attention_b",
    )(rel_bias, *bks, *([qkv3] * 9), gq, gk)


def kernel(x, norm1_g, w_in, b_gate, q_norm_a, k_norm_a, q_norm_b, k_norm_b, rel_bias,
           w_proj_a, w_proj_b, w_out, norm2_g, w_ffn_gate, w_ffn_up, w_ffn_down):
    b, s, d = x.shape
    m = b * s
    a_w = (N_A_HEADS + 2 * N_A_KV_HEADS) * HEAD_DIM
    b_w = 3 * N_B_GROUPS * N_B_HEADS_PER_GROUP * HEAD_DIM
    cos, sin_signed = _rope_tables(s)
    d_ff = w_ffn_gate.shape[-1]
    tk_down = d_ff // 2

    xf = x.reshape(m, d)
    for l in range(norm1_g.shape[0]):
        h = _rmsnorm(xf, norm1_g[l], name="rmsnorm1")
        qkv_a = _matmul(h, w_in[l], col0=0, ncols=a_w, tm=2048, tn=512, out_dtype=BF16,
                        name="in_proj_a")
        qkv_b = _matmul(h, w_in[l], col0=a_w, ncols=b_w, tm=2048, tn=512, out_dtype=F32,
                        name="in_proj_b")
        gates = _matmul(h, w_in[l], col0=a_w + b_w, ncols=2 * d, tm=2048, tn=512, out_dtype=BF16,
                        name="in_proj_gates")

        o_a = _attention_a(qkv_a.reshape(b, s, a_w), cos, sin_signed, q_norm_a[l].reshape(1, -1),
                           k_norm_a[l].reshape(1, -1), tq=256)
        o_b = _attention_b(qkv_b.reshape(b, s, b_w), rel_bias, q_norm_b[l].reshape(1, -1),
                           k_norm_b[l].reshape(1, -1))

        merged = _merge(o_a.reshape(m, -1), o_b.reshape(m, -1), gates, b_gate[l], w_proj_a[l],
                        w_proj_b[l], tm=2048, tn=512, name="gated_merge")
        xf, xg, rs = _matmul_resid_norm(merged, w_out[l], xf, norm2_g[l], tm=2048, tn=256,
                                        rs_width=HEAD_DIM, name="out_proj")
        act = _swiglu(xg, rs, w_ffn_gate[l], w_ffn_up[l], tm=2048, tn=256, name="swiglu_up")
        xf = _matmul_resid_rows(act, w_ffn_down[l], xf, tm=1024, tn=256,
                                vmem_limit=62 * 1024 * 1024, name="ffn_down")
    return xf.reshape(b, s, d)
```

```python
import functools
import math

import jax
import jax.numpy as jnp
import numpy as np
from jax import lax
from jax.experimental import pallas as pl
from jax.experimental.pallas import tpu as pltpu

F32 = jnp.float32
BF16 = jnp.bfloat16

HEAD_DIM = 128
N_A_HEADS = 16
N_A_KV_HEADS = 4
N_A_GROUP = N_A_HEADS // N_A_KV_HEADS
B_PATTERNS = ((128, 1), (512, 4), (2048, 16))
N_B_GROUPS = len(B_PATTERNS)
N_B_HEADS_PER_GROUP = 4
GRID_W = 64
ROPE_THETA = 10000.0
AXIS_ROPE_DIM = HEAD_DIM // 2
REL_BUCKETS = 32
REL_MAX_DIST = 1024
EPS = 1e-6
NEG_INF = -1e30
ATTN_SCALE = HEAD_DIM ** -0.5

Q_TILE = 128
HEAD_BLOCK = N_A_GROUP * HEAD_DIM
VMEM_LIMIT = 62 * 1024 * 1024


def _cparams(sem, vmem_limit=VMEM_LIMIT):
    return pltpu.CompilerParams(dimension_semantics=sem, vmem_limit_bytes=vmem_limit)


def _rmsnorm_kernel(x_ref, g_ref, o_ref):
    x = x_ref[...]
    ms = jnp.mean(x * x, axis=-1, keepdims=True)
    o_ref[...] = (x * lax.rsqrt(ms + EPS) * g_ref[...]).astype(o_ref.dtype)


def _rmsnorm(x, g, *, tm=512, name):
    m, d = x.shape
    return pl.pallas_call(
        _rmsnorm_kernel,
        grid=(m // tm,),
        in_specs=[pl.BlockSpec((tm, d), lambda i: (i, 0)),
                  pl.BlockSpec((1, d), lambda i: (0, 0))],
        out_specs=pl.BlockSpec((tm, d), lambda i: (i, 0)),
        out_shape=jax.ShapeDtypeStruct((m, d), BF16),
        compiler_params=_cparams(("parallel",)),
        name=name,
    )(x, g.reshape(1, d))


ROW_CHUNKS = 2


def _row_spec(tm, k, single_buffered=False):
    if single_buffered:
        return pl.BlockSpec((tm, k), lambda i, j: (i, 0), pipeline_mode=pl.Buffered(1))
    return pl.BlockSpec((tm, k), lambda i, j: (i, 0))


def _mm_kernel(a_ref, w_ref, o_ref):
    w = w_ref[...].astype(BF16)
    o_ref[...] = jnp.dot(a_ref[...], w, preferred_element_type=F32).astype(o_ref.dtype)


def _matmul(a, w, *, col0, ncols, tm, tn, out_dtype, name):
    m, k = a.shape
    jb = col0 // tn
    return pl.pallas_call(
        _mm_kernel,
        grid=(m // tm, ncols // tn),
        in_specs=[_row_spec(tm, k),
                  pl.BlockSpec((k, tn), lambda i, j: (0, jb + j))],
        out_specs=pl.BlockSpec((tm, tn), lambda i, j: (i, j)),
        out_shape=jax.ShapeDtypeStruct((m, ncols), out_dtype),
        compiler_params=_cparams(("arbitrary", "arbitrary")),
        name=name,
    )(a, w)


def _mm_resid_norm_kernel(a_ref, w_ref, r_ref, g_ref, o_ref, og_ref, rs_ref, ss_ref, *, width):
    j = pl.program_id(1)
    w = w_ref[...].astype(BF16)
    g = g_ref[...]
    tm = a_ref.shape[0]
    chunk = tm // ROW_CHUNKS
    accs = [jnp.dot(a_ref[c * chunk:(c + 1) * chunk, :], w, preferred_element_type=F32)
            for c in range(ROW_CHUNKS)]
    for c, acc in enumerate(accs):
        rows = slice(c * chunk, (c + 1) * chunk)
        y = r_ref[rows, :] + acc
        o_ref[rows, :] = y
        og_ref[rows, :] = (y * g).astype(og_ref.dtype)
        sq = y * y
        part = sq[:, :HEAD_DIM]
        for cc in range(1, sq.shape[1] // HEAD_DIM):
            part = part + sq[:, cc * HEAD_DIM:(cc + 1) * HEAD_DIM]
        ss_ref[rows, :] = jnp.where(j == 0, part, ss_ref[rows, :] + part)

    @pl.when(j == pl.num_programs(1) - 1)
    def _():
        ms = jnp.sum(ss_ref[...], axis=-1, keepdims=True) * (1.0 / width)
        rs_ref[...] = jnp.broadcast_to(lax.rsqrt(ms + EPS), rs_ref.shape)


def _matmul_resid_norm(a, w, resid, gain, *, tm, tn, rs_width, name):
    m, k = a.shape
    n = w.shape[1]
    return pl.pallas_call(
        functools.partial(_mm_resid_norm_kernel, width=n),
        grid=(m // tm, n // tn),
        in_specs=[_row_spec(tm, k, single_buffered=True),
                  pl.BlockSpec((k, tn), lambda i, j: (0, j)),
                  pl.BlockSpec((tm, tn), lambda i, j: (i, j)),
                  pl.BlockSpec((1, tn), lambda i, j: (0, j))],
        out_specs=[pl.BlockSpec((tm, tn), lambda i, j: (i, j)),
                   pl.BlockSpec((tm, tn), lambda i, j: (i, j)),
                   pl.BlockSpec((tm, rs_width), lambda i, j: (i, 0))],
        out_shape=[jax.ShapeDtypeStruct((m, n), F32),
                   jax.ShapeDtypeStruct((m, n), BF16),
                   jax.ShapeDtypeStruct((m, rs_width), F32)],
        scratch_shapes=[pltpu.VMEM((tm, HEAD_DIM), F32)],
        compiler_params=_cparams(("arbitrary", "arbitrary")),
        name=name,
    )(a, w, resid, gain.reshape(1, n))


def _mm_resid_rows_kernel(a_ref, w_ref, r_ref, o_ref):
    w = w_ref[...].astype(BF16)
    o_ref[...] = r_ref[...] + jnp.dot(a_ref[...], w, preferred_element_type=F32)


def _matmul_resid_rows(a, w, resid, *, tm, tn, vmem_limit, name):
    m, k = a.shape
    n = w.shape[1]
    return pl.pallas_call(
        _mm_resid_rows_kernel,
        grid=(m // tm, n // tn),
        in_specs=[_row_spec(tm, k, single_buffered=True),
                  pl.BlockSpec((k, tn), lambda i, j: (0, j)),
                  pl.BlockSpec((tm, tn), lambda i, j: (i, j))],
        out_specs=pl.BlockSpec((tm, tn), lambda i, j: (i, j)),
        out_shape=jax.ShapeDtypeStruct((m, n), F32),
        compiler_params=_cparams(("arbitrary", "arbitrary"), vmem_limit),
        name=name,
    )(a, w, resid)


def _mm_resid_kernel(a_ref, w_ref, r_ref, o_ref, wb_ref):
    @pl.when(pl.program_id(1) == 0)
    def _():
        wb_ref[...] = w_ref[...].astype(BF16)

    o_ref[...] = r_ref[...] + jnp.dot(a_ref[...], wb_ref[...], preferred_element_type=F32)


def _matmul_resid(a, w, resid, *, tm, tn, tk, kblock, name):
    m = a.shape[0]
    n = w.shape[1]
    return pl.pallas_call(
        _mm_resid_kernel,
        grid=(n // tn, m // tm),
        in_specs=[pl.BlockSpec((tm, tk), lambda j, i: (i, kblock)),
                  pl.BlockSpec((tk, tn), lambda j, i: (kblock, j)),
                  pl.BlockSpec((tm, tn), lambda j, i: (i, j))],
        out_specs=pl.BlockSpec((tm, tn), lambda j, i: (i, j)),
        out_shape=jax.ShapeDtypeStruct((m, n), F32),
        scratch_shapes=[pltpu.VMEM((tk, tn), BF16)],
        compiler_params=_cparams(("arbitrary", "arbitrary")),
        name=name,
    )(a, w, resid)


def _merge_kernel(oa_ref, ob_ref, ga_ref, gb_ref, bg_ref, wa_ref, wb_ref, o_ref):
    wa = wa_ref[...].astype(BF16)
    wb = wb_ref[...].astype(BF16)
    chunk = oa_ref.shape[0] // ROW_CHUNKS
    prods = []
    for c in range(ROW_CHUNKS):
        rows = slice(c * chunk, (c + 1) * chunk)
        prods.append((jnp.dot(oa_ref[rows, :], wa, preferred_element_type=F32),
                      jnp.dot(ob_ref[rows, :], wb, preferred_element_type=F32)))
    for c, (pa, pb) in enumerate(prods):
        rows = slice(c * chunk, (c + 1) * chunk)
        gate_a = jax.nn.sigmoid(ga_ref[rows, :].astype(F32) + bg_ref[0:1, :])
        gate_b = jax.nn.sigmoid(gb_ref[rows, :].astype(F32) + bg_ref[1:2, :])
        o_ref[rows, :] = (gate_a * pa + gate_b * pb).astype(o_ref.dtype)


def _merge(o_a, o_b, gates, b_gate, w_proj_a, w_proj_b, *, tm, tn, name):
    m, ka = o_a.shape
    kb = o_b.shape[1]
    n = w_proj_a.shape[1]
    gb_blk = n // tn
    return pl.pallas_call(
        _merge_kernel,
        grid=(m // tm, n // tn),
        in_specs=[_row_spec(tm, ka),
                  _row_spec(tm, kb),
                  pl.BlockSpec((tm, tn), lambda i, j: (i, j)),
                  pl.BlockSpec((tm, tn), lambda i, j: (i, gb_blk + j)),
                  pl.BlockSpec((2, tn), lambda i, j: (0, j)),
                  pl.BlockSpec((ka, tn), lambda i, j: (0, j)),
                  pl.BlockSpec((kb, tn), lambda i, j: (0, j))],
        out_specs=pl.BlockSpec((tm, tn), lambda i, j: (i, j)),
        out_shape=jax.ShapeDtypeStruct((m, n), BF16),
        compiler_params=_cparams(("arbitrary", "arbitrary")),
        name=name,
    )(o_a, o_b, gates, gates, b_gate, w_proj_a, w_proj_b)


def _swiglu_kernel(h_ref, rs_ref, wg_ref, wu_ref, o_ref):
    wg = wg_ref[...].astype(BF16)
    wu = wu_ref[...].astype(BF16)
    chunk = h_ref.shape[0] // ROW_CHUNKS
    prods = []
    for c in range(ROW_CHUNKS):
        h = h_ref[c * chunk:(c + 1) * chunk, :]
        prods.append((jnp.dot(h, wg, preferred_element_type=F32),
                      jnp.dot(h, wu, preferred_element_type=F32)))
    for c, (g, u) in enumerate(prods):
        rows = slice(c * chunk, (c + 1) * chunk)
        rs = rs_ref[rows, :]
        rs = jnp.concatenate([rs] * (o_ref.shape[1] // rs.shape[1]), axis=1)
        g = g * rs
        o_ref[rows, :] = (g * jax.nn.sigmoid(g) * (u * rs)).astype(o_ref.dtype)


def _swiglu(h, row_scale, w_gate, w_up, *, tm, tn, name):
    m, k = h.shape
    n = w_gate.shape[1]
    return pl.pallas_call(
        _swiglu_kernel,
        grid=(m // tm, n // tn),
        in_specs=[_row_spec(tm, k),
                  pl.BlockSpec((tm, row_scale.shape[1]), lambda i, j: (i, 0)),
                  pl.BlockSpec((k, tn), lambda i, j: (0, j)),
                  pl.BlockSpec((k, tn), lambda i, j: (0, j))],
        out_specs=pl.BlockSpec((tm, tn), lambda i, j: (i, j)),
        out_shape=jax.ShapeDtypeStruct((m, n), BF16),
        compiler_params=_cparams(("arbitrary", "arbitrary")),
        name=name,
    )(h, row_scale, w_gate, w_up)


def _head_rms(x, g):
    ms = jnp.mean(x * x, axis=-1, keepdims=True)
    return x * lax.rsqrt(ms + EPS) * g


def _rope(x, cos, sin_signed):
    lane = lax.broadcasted_iota(jnp.int32, x.shape, 1)
    first_half = (lane % AXIS_ROPE_DIM) < (AXIS_ROPE_DIM // 2)
    partner = jnp.where(first_half,
                        pltpu.roll(x, HEAD_DIM - AXIS_ROPE_DIM // 2, 1),
                        pltpu.roll(x, AXIS_ROPE_DIM // 2, 1))
    return x * cos + partner * sin_signed


def _rope_tables(seq):
    rows = seq // GRID_W
    row = jnp.broadcast_to(jnp.arange(rows)[:, None], (rows, GRID_W)).reshape(-1).astype(F32)
    col = jnp.broadcast_to(jnp.arange(GRID_W)[None, :], (rows, GRID_W)).reshape(-1).astype(F32)
    inv = ROPE_THETA ** (-jnp.arange(0, AXIS_ROPE_DIM, 2, dtype=F32) / AXIS_ROPE_DIM)
    ang_row, ang_col = row[:, None] * inv, col[:, None] * inv
    cr, sr, cc, sc = jnp.cos(ang_row), jnp.sin(ang_row), jnp.cos(ang_col), jnp.sin(ang_col)
    cos = jnp.concatenate([cr, cr, cc, cc], axis=-1)
    sin_signed = jnp.concatenate([-sr, sr, -sc, sc], axis=-1)
    return cos, sin_signed


SOFTMAX_ROWS = 16
LOG2E = math.log2(math.e)


def _attn_a_kernel(q_ref, k_ref, v_ref, cq_ref, sq_ref, ck_ref, sk_ref, gq_ref, gk_ref,
                   o_ref, kn_ref, v1_ref, s_ref, p_ref):
    @pl.when(pl.program_id(2) == 0)
    def _():
        k = _head_rms(k_ref[0].astype(F32), gk_ref[...])
        kn_ref[...] = _rope(k, ck_ref[...], sk_ref[...]).astype(BF16)
        v1_ref[:, :HEAD_DIM] = v_ref[0]
        v1_ref[:, HEAD_DIM:] = jnp.ones((v_ref.shape[1], HEAD_DIM), BF16)

    kn = kn_ref[...]
    v1 = v1_ref[...]
    cq, sq, gq = cq_ref[...], sq_ref[...], gq_ref[...]
    tq = q_ref.shape[1]
    def scores(g):
        q = _head_rms(q_ref[0, :, g * HEAD_DIM:(g + 1) * HEAD_DIM].astype(F32), gq)
        q = (_rope(q, cq, sq) * (ATTN_SCALE * LOG2E)).astype(BF16)
        s_ref[g % 2] = lax.dot_general(q, kn, (((1,), (1,)), ((), ())),
                                       preferred_element_type=F32)

    def softmax(g):
        slot = g % 2
        for r in range(tq // SOFTMAX_ROWS):
            rows = pl.ds(r * SOFTMAX_ROWS, SOFTMAX_ROWS)
            s = s_ref[slot, rows, :]
            p_ref[slot, rows, :] = jnp.exp2(s - jnp.max(s, axis=-1, keepdims=True)).astype(BF16)

    def weighted_values(g):
        ov = jnp.dot(p_ref[g % 2], v1, preferred_element_type=F32)
        o_ref[0, :, g * HEAD_DIM:(g + 1) * HEAD_DIM] = (
            ov[:, :HEAD_DIM] / ov[:, HEAD_DIM:]).astype(o_ref.dtype)

    scores(0)
    for g in range(N_A_GROUP):
        if g + 1 < N_A_GROUP:
            scores(g + 1)
        softmax(g)
        weighted_values(g)


def _attention_a(qkv3, cos, sin_signed, gq, gk, *, tq):
    b, s, _ = qkv3.shape
    kb = N_A_HEADS
    vb = N_A_HEADS + N_A_KV_HEADS
    return pl.pallas_call(
        _attn_a_kernel,
        grid=(b, N_A_KV_HEADS, s // tq),
        in_specs=[pl.BlockSpec((1, tq, HEAD_BLOCK), lambda bi, kv, qi: (bi, qi, kv)),
                  pl.BlockSpec((1, s, HEAD_DIM), lambda bi, kv, qi: (bi, 0, kb + kv)),
                  pl.BlockSpec((1, s, HEAD_DIM), lambda bi, kv, qi: (bi, 0, vb + kv)),
                  pl.BlockSpec((tq, HEAD_DIM), lambda bi, kv, qi: (qi, 0)),
                  pl.BlockSpec((tq, HEAD_DIM), lambda bi, kv, qi: (qi, 0)),
                  pl.BlockSpec((s, HEAD_DIM), lambda bi, kv, qi: (0, 0)),
                  pl.BlockSpec((s, HEAD_DIM), lambda bi, kv, qi: (0, 0)),
                  pl.BlockSpec((1, HEAD_DIM), lambda bi, kv, qi: (0, 0)),
                  pl.BlockSpec((1, HEAD_DIM), lambda bi, kv, qi: (0, 0))],
        out_specs=pl.BlockSpec((1, tq, HEAD_BLOCK), lambda bi, kv, qi: (bi, qi, kv)),
        out_shape=jax.ShapeDtypeStruct((b, s, N_A_HEADS * HEAD_DIM), BF16),
        scratch_shapes=[pltpu.VMEM((s, HEAD_DIM), BF16),
                        pltpu.VMEM((s, 2 * HEAD_DIM), BF16),
                        pltpu.VMEM((2, tq, s), F32),
                        pltpu.VMEM((2, tq, s), BF16)],
        compiler_params=_cparams(("arbitrary", "arbitrary", "arbitrary")),
        name="attention_a",
    )(qkv3, qkv3, qkv3, cos, sin_signed, cos, sin_signed, gq, gk)


RADIUS = 64
assert all(w // (2 * d) == RADIUS for w, d in B_PATTERNS)
SCORES_AHEAD = 1


def _t5_bucket_np(rel):
    nb = REL_BUCKETS // 2
    max_exact = nb // 2
    side = np.where(rel > 0, nb, 0)
    n = np.abs(rel)
    nf = np.maximum(n, 1).astype(np.float32)
    large = max_exact + (np.log(nf / np.float32(max_exact))
                         / np.float32(math.log(REL_MAX_DIST / max_exact))
                         * np.float32(nb - max_exact)).astype(np.int32)
    large = np.minimum(large, nb - 1)
    return (side + np.where(n < max_exact, n, large)).astype(np.int32)


def _band_geometry(seq, dil):
    length = seq // dil
    kw = min(2 * Q_TILE, length)
    nblk = length // Q_TILE
    starts = [min(max(mb * Q_TILE - RADIUS, 0), length - kw) for mb in range(nblk)]
    shifts = sorted({st - mb * Q_TILE for mb, st in enumerate(starts)}, reverse=True)
    return length, kw, nblk, shifts


def _bucket_tiles(seq, dil):
    _, kw, _, shifts = _band_geometry(seq, dil)
    i = np.arange(Q_TILE)[:, None]
    j = np.arange(kw)[None, :]
    tiles = []
    for shift in shifts:
        rel = j - i + shift
        tiles.append(np.where(np.abs(rel) <= RADIUS, _t5_bucket_np(rel * dil), -1))
    return np.stack(tiles, axis=0).astype(np.int32)


def _attn_b_kernel(rb_ref, bk0_ref, bk1_ref, bk2_ref,
                   q0_ref, q1_ref, q2_ref, k0_ref, k1_ref, k2_ref, v0_ref, v1_ref, v2_ref,
                   gq_ref, gk_ref, o_ref,
                   qc_ref, kc_ref, vc_ref, on_ref, ln_ref, t0_ref, t1_ref, t2_ref, *, seq):
    head = pl.program_id(0)
    gq, gk = gq_ref[...], gk_ref[...]
    bk_refs = (bk0_ref, bk1_ref, bk2_ref)
    tile_refs = (t0_ref, t1_ref, t2_ref)
    q_refs, k_refs, v_refs = (q0_ref, q1_ref, q2_ref), (k0_ref, k1_ref, k2_ref), (v0_ref, v1_ref, v2_ref)

    @pl.when(pl.program_id(1) == 0)
    def _():
        for g in range(N_B_GROUPS):
            col = g * N_B_HEADS_PER_GROUP + head
            for var in range(bk_refs[g].shape[0]):
                bkt = bk_refs[g][var]

                def pick(b, acc, bkt=bkt, col=col):
                    return jnp.where(bkt == b, rb_ref[b, col], acc)

                tile_refs[g][var] = lax.fori_loop(0, REL_BUCKETS, pick,
                                                  jnp.full(bkt.shape, NEG_INF, F32))

    for g, (_, dil) in enumerate(B_PATTERNS):
        length, kw, nblk, shifts = _band_geometry(seq, dil)

        for c in range(dil):
            rows = pl.ds(c, length, stride=dil) if dil > 1 else pl.ds(0, length)
            dst = pl.ds(c * length, length)
            qc_ref[g, dst, :] = (_head_rms(q_refs[g][0, rows, :], gq) * ATTN_SCALE).astype(BF16)
            kc_ref[g, dst, :] = _head_rms(k_refs[g][0, rows, :], gk).astype(BF16)
            vc_ref[g, dst, :] = v_refs[g][0, rows, :].astype(BF16)

        def key_rows(c, mb):
            start = min(max(mb * Q_TILE - RADIUS, 0), length - kw)
            return pl.ds(c * length + start, kw), shifts.index(start - mb * Q_TILE)

        def scores(c, mb):
            krows, var = key_rows(c, mb)
            q = qc_ref[g, pl.ds(c * length + mb * Q_TILE, Q_TILE), :]
            logits = lax.dot_general(q, kc_ref[g, krows, :], (((1,), (1,)), ((), ())),
                                     preferred_element_type=F32)
            return logits + tile_refs[g][var]

        def finish(c, mb, logits):
            krows, _ = key_rows(c, mb)
            m = jnp.max(logits, axis=-1, keepdims=True)
            p = jnp.exp(logits - m)
            denom = jnp.sum(p, axis=-1, keepdims=True)
            o = jnp.dot(p.astype(BF16), vc_ref[g, krows, :], preferred_element_type=F32) / denom
            t0 = c + dil * mb * Q_TILE
            rows = pl.ds(t0, Q_TILE, stride=dil) if dil > 1 else pl.ds(t0, Q_TILE)
            on_ref[g, rows, :] = o
            ln_ref[g, rows, :] = jnp.broadcast_to(m + jnp.log(denom), (Q_TILE, HEAD_DIM))

        blocks = [(c, mb) for c in range(dil) for mb in range(nblk)]
        ahead = [scores(*blk) for blk in blocks[:SCORES_AHEAD]]
        for i, blk in enumerate(blocks):
            if i + SCORES_AHEAD < len(blocks):
                ahead.append(scores(*blocks[i + SCORES_AHEAD]))
            finish(*blk, ahead.pop(0))

    chunk = 2 * Q_TILE

    def combine(i, carry):
        rows = pl.ds(pl.multiple_of(i * chunk, chunk), chunk)
        l0, l1, l2 = ln_ref[0, rows, :], ln_ref[1, rows, :], ln_ref[2, rows, :]
        mx = jnp.maximum(jnp.maximum(l0, l1), l2)
        e0, e1, e2 = jnp.exp(l0 - mx), jnp.exp(l1 - mx), jnp.exp(l2 - mx)
        num = e0 * on_ref[0, rows, :] + e1 * on_ref[1, rows, :] + e2 * on_ref[2, rows, :]
        o_ref[0, rows, :] = (num / (e0 + e1 + e2)).astype(o_ref.dtype)
        return carry

    lax.fori_loop(0, seq // chunk, combine, 0)


def _attention_b(qkv3, rel_bias, gq, gk):
    b, s, _ = qkv3.shape
    nh = N_B_HEADS_PER_GROUP
    bks = [jnp.asarray(_bucket_tiles(s, dil)) for _, dil in B_PATTERNS]

    def head_spec(part, g):
        base = (part * N_B_GROUPS + g) * nh
        return pl.BlockSpec((1, s, HEAD_DIM), lambda h, bi: (bi, 0, base + h))

    def full_spec(arr):
        nd = arr.ndim
        return pl.BlockSpec(arr.shape, lambda h, bi: (0,) * nd)

    in_specs = [pl.BlockSpec(memory_space=pltpu.SMEM)]
    in_specs += [full_spec(bk) for bk in bks]
    in_specs += [head_spec(part, g) for part in range(3) for g in range(N_B_GROUPS)]
    in_specs += [pl.BlockSpec((1, HEAD_DIM), lambda h, bi: (0, 0))] * 2
    scratch = [pltpu.VMEM((N_B_GROUPS, s, HEAD_DIM), BF16)] * 3
    scratch += [pltpu.VMEM((N_B_GROUPS, s, HEAD_DIM), F32)] * 2
    scratch += [pltpu.VMEM(bk.shape, F32) for bk in bks]
    return pl.pallas_call(
        functools.partial(_attn_b_kernel, seq=s),
        grid=(nh, b),
        in_specs=in_specs,
        out_specs=pl.BlockSpec((1, s, HEAD_DIM), lambda h, bi: (bi, 0, h)),
        out_shape=jax.ShapeDtypeStruct((b, s, nh * HEAD_DIM), BF16),
        scratch_shapes=scratch,
        compiler_params=_cparams(("arbitrary", "arbitrary")),
        name="attention_b",
    )(rel_bias, *bks, *([qkv3] * 9), gq, gk)


def kernel(x, norm1_g, w_in, b_gate, q_norm_a, k_norm_a, q_norm_b, k_norm_b, rel_bias,
           w_proj_a, w_proj_b, w_out, norm2_g, w_ffn_gate, w_ffn_up, w_ffn_down):
    b, s, d = x.shape
    m = b * s
    a_w = (N_A_HEADS + 2 * N_A_KV_HEADS) * HEAD_DIM
    b_w = 3 * N_B_GROUPS * N_B_HEADS_PER_GROUP * HEAD_DIM
    cos, sin_signed = _rope_tables(s)
    d_ff = w_ffn_gate.shape[-1]
    tk_down = d_ff // 2

    xf = x.reshape(m, d)
    for l in range(norm1_g.shape[0]):
        h = _rmsnorm(xf, norm1_g[l], name="rmsnorm1")
        qkv_a = _matmul(h, w_in[l], col0=0, ncols=a_w, tm=2048, tn=512, out_dtype=BF16,
                        name="in_proj_a")
        qkv_b = _matmul(h, w_in[l], col0=a_w, ncols=b_w, tm=2048, tn=512, out_dtype=F32,
                        name="in_proj_b")
        gates = _matmul(h, w_in[l], col0=a_w + b_w, ncols=2 * d, tm=2048, tn=512, out_dtype=BF16,
                        name="in_proj_gates")

        o_a = _attention_a(qkv_a.reshape(b, s, a_w), cos, sin_signed, q_norm_a[l].reshape(1, -1),
                           k_norm_a[l].reshape(1, -1), tq=256)
        o_b = _attention_b(qkv_b.reshape(b, s, b_w), rel_bias, q_norm_b[l].reshape(1, -1),
                           k_norm_b[l].reshape(1, -1))

        merged = _merge(o_a.reshape(m, -1), o_b.reshape(m, -1), gates, b_gate[l], w_proj_a[l],
                        w_proj_b[l], tm=2048, tn=512, name="gated_merge")
        xf, xg, rs = _matmul_resid_norm(merged, w_out[l], xf, norm2_g[l], tm=2048, tn=256,
                                        rs_width=HEAD_DIM, name="out_proj")
        act = _swiglu(xg, rs, w_ffn_gate[l], w_ffn_up[l], tm=2048, tn=256, name="swiglu_up")
        xf = _matmul_resid_rows(act, w_ffn_down[l], xf, tm=1024, tn=256,
                                vmem_limit=62 * 1024 * 1024, name="ffn_down")
    return xf.reshape(b, s, d)
```

```python
import functools
import math

import jax
import jax.numpy as jnp
import numpy as np
from jax import lax
from jax.experimental import pallas as pl
from jax.experimental.pallas import tpu as pltpu

F32 = jnp.float32
BF16 = jnp.bfloat16

HEAD_DIM = 128
N_A_HEADS = 16
N_A_KV_HEADS = 4
N_A_GROUP = N_A_HEADS // N_A_KV_HEADS
B_PATTERNS = ((128, 1), (512, 4), (2048, 16))
N_B_GROUPS = len(B_PATTERNS)
N_B_HEADS_PER_GROUP = 4
GRID_W = 64
ROPE_THETA = 10000.0
AXIS_ROPE_DIM = HEAD_DIM // 2
REL_BUCKETS = 32
REL_MAX_DIST = 1024
EPS = 1e-6
NEG_INF = -1e30
ATTN_SCALE = HEAD_DIM ** -0.5

Q_TILE = 128
HEAD_BLOCK = N_A_GROUP * HEAD_DIM
VMEM_LIMIT = 62 * 1024 * 1024


def _cparams(sem, vmem_limit=VMEM_LIMIT):
    return pltpu.CompilerParams(dimension_semantics=sem, vmem_limit_bytes=vmem_limit)


def _rmsnorm_kernel(x_ref, g_ref, o_ref):
    x = x_ref[...]
    ms = jnp.mean(x * x, axis=-1, keepdims=True)
    o_ref[...] = (x * lax.rsqrt(ms + EPS) * g_ref[...]).astype(o_ref.dtype)


def _rmsnorm(x, g, *, tm=512, name):
    m, d = x.shape
    return pl.pallas_call(
        _rmsnorm_kernel,
        grid=(m // tm,),
        in_specs=[pl.BlockSpec((tm, d), lambda i: (i, 0)),
                  pl.BlockSpec((1, d), lambda i: (0, 0))],
        out_specs=pl.BlockSpec((tm, d), lambda i: (i, 0)),
        out_shape=jax.ShapeDtypeStruct((m, d), BF16),
        compiler_params=_cparams(("parallel",)),
        name=name,
    )(x, g.reshape(1, d))


ROW_CHUNKS = 2


def _row_spec(tm, k, single_buffered=False):
    if single_buffered:
        return pl.BlockSpec((tm, k), lambda i, j: (i, 0), pipeline_mode=pl.Buffered(1))
    return pl.BlockSpec((tm, k), lambda i, j: (i, 0))


def _mm_kernel(a_ref, w_ref, o_ref):
    w = w_ref[...].astype(BF16)
    o_ref[...] = jnp.dot(a_ref[...], w, preferred_element_type=F32).astype(o_ref.dtype)


def _matmul(a, w, *, col0, ncols, tm, tn, out_dtype, name):
    m, k = a.shape
    jb = col0 // tn
    return pl.pallas_call(
        _mm_kernel,
        grid=(m // tm, ncols // tn),
        in_specs=[_row_spec(tm, k),
                  pl.BlockSpec((k, tn), lambda i, j: (0, jb + j))],
        out_specs=pl.BlockSpec((tm, tn), lambda i, j: (i, j)),
        out_shape=jax.ShapeDtypeStruct((m, ncols), out_dtype),
        compiler_params=_cparams(("arbitrary", "arbitrary")),
        name=name,
    )(a, w)


def _mm_resid_norm_kernel(a_ref, w_ref, r_ref, g_ref, o_ref, og_ref, rs_ref, ss_ref, *, width):
    j = pl.program_id(1)
    w = w_ref[...].astype(BF16)
    g = g_ref[...]
    tm = a_ref.shape[0]
    chunk = tm // ROW_CHUNKS
    accs = [jnp.dot(a_ref[c * chunk:(c + 1) * chunk, :], w, preferred_element_type=F32)
            for c in range(ROW_CHUNKS)]
    for c, acc in enumerate(accs):
        rows = slice(c * chunk, (c + 1) * chunk)
        y = r_ref[rows, :] + acc
        o_ref[rows, :] = y
        og_ref[rows, :] = (y * g).astype(og_ref.dtype)
        sq = y * y
        part = sq[:, :HEAD_DIM]
        for cc in range(1, sq.shape[1] // HEAD_DIM):
            part = part + sq[:, cc * HEAD_DIM:(cc + 1) * HEAD_DIM]
        ss_ref[rows, :] = jnp.where(j == 0, part, ss_ref[rows, :] + part)

    @pl.when(j == pl.num_programs(1) - 1)
    def _():
        ms = jnp.sum(ss_ref[...], axis=-1, keepdims=True) * (1.0 / width)
        rs_ref[...] = jnp.broadcast_to(lax.rsqrt(ms + EPS), rs_ref.shape)


def _matmul_resid_norm(a, w, resid, gain, *, tm, tn, rs_width, name):
    m, k = a.shape
    n = w.shape[1]
    return pl.pallas_call(
        functools.partial(_mm_resid_norm_kernel, width=n),
        grid=(m // tm, n // tn),
        in_specs=[_row_spec(tm, k, single_buffered=True),
                  pl.BlockSpec((k, tn), lambda i, j: (0, j)),
                  pl.BlockSpec((tm, tn), lambda i, j: (i, j)),
                  pl.BlockSpec((1, tn), lambda i, j: (0, j))],
        out_specs=[pl.BlockSpec((tm, tn), lambda i, j: (i, j)),
                   pl.BlockSpec((tm, tn), lambda i, j: (i, j)),
                   pl.BlockSpec((tm, rs_width), lambda i, j: (i, 0))],
        out_shape=[jax.ShapeDtypeStruct((m, n), F32),
                   jax.ShapeDtypeStruct((m, n), BF16),
                   jax.ShapeDtypeStruct((m, rs_width), F32)],
        scratch_shapes=[pltpu.VMEM((tm, HEAD_DIM), F32)],
        compiler_params=_cparams(("arbitrary", "arbitrary")),
        name=name,
    )(a, w, resid, gain.reshape(1, n))


def _mm_resid_rows_kernel(a_ref, w_ref, r_ref, o_ref):
    w = w_ref[...].astype(BF16)
    o_ref[...] = r_ref[...] + jnp.dot(a_ref[...], w, preferred_element_type=F32)


def _matmul_resid_rows(a, w, resid, *, tm, tn, vmem_limit, name):
    m, k = a.shape
    n = w.shape[1]
    return pl.pallas_call(
        _mm_resid_rows_kernel,
        grid=(m // tm, n // tn),
        in_specs=[_row_spec(tm, k, single_buffered=True),
                  pl.BlockSpec((k, tn), lambda i, j: (0, j)),
                  pl.BlockSpec((tm, tn), lambda i, j: (i, j))],
        out_specs=pl.BlockSpec((tm, tn), lambda i, j: (i, j)),
        out_shape=jax.ShapeDtypeStruct((m, n), F32),
        compiler_params=_cparams(("arbitrary", "arbitrary"), vmem_limit),
        name=name,
    )(a, w, resid)


def _mm_resid_kernel(a_ref, w_ref, r_ref, o_ref, wb_ref):
    @pl.when(pl.program_id(1) == 0)
    def _():
        wb_ref[...] = w_ref[...].astype(BF16)

    o_ref[...] = r_ref[...] + jnp.dot(a_ref[...], wb_ref[...], preferred_element_type=F32)


def _matmul_resid(a, w, resid, *, tm, tn, tk, kblock, name):
    m = a.shape[0]
    n = w.shape[1]
    return pl.pallas_call(
        _mm_resid_kernel,
        grid=(n // tn, m // tm),
        in_specs=[pl.BlockSpec((tm, tk), lambda j, i: (i, kblock)),
                  pl.BlockSpec((tk, tn), lambda j, i: (kblock, j)),
                  pl.BlockSpec((tm, tn), lambda j, i: (i, j))],
        out_specs=pl.BlockSpec((tm, tn), lambda j, i: (i, j)),
        out_shape=jax.ShapeDtypeStruct((m, n), F32),
        scratch_shapes=[pltpu.VMEM((tk, tn), BF16)],
        compiler_params=_cparams(("arbitrary", "arbitrary")),
        name=name,
    )(a, w, resid)


def _merge_kernel(oa_ref, ob_ref, ga_ref, gb_ref, bg_ref, wa_ref, wb_ref, o_ref):
    wa = wa_ref[...].astype(BF16)
    wb = wb_ref[...].astype(BF16)
    chunk = oa_ref.shape[0] // ROW_CHUNKS
    prods = []
    for c in range(ROW_CHUNKS):
        rows = slice(c * chunk, (c + 1) * chunk)
        pb = jnp.dot(ob_ref[rows, :], wb, preferred_element_type=F32)
        gate_b = jax.nn.sigmoid(gb_ref[rows, :].astype(F32) + bg_ref[1:2, :])
        pa = jnp.dot(oa_ref[rows, :], wa, preferred_element_type=F32)
        prods.append((pa, gate_b * pb))
    for c, (pa, gpb) in enumerate(prods):
        rows = slice(c * chunk, (c + 1) * chunk)
        gate_a = jax.nn.sigmoid(ga_ref[rows, :].astype(F32) + bg_ref[0:1, :])
        o_ref[rows, :] = (gate_a * pa + gpb).astype(o_ref.dtype)


def _merge(o_a, o_b, gates, b_gate, w_proj_a, w_proj_b, *, tm, tn, name):
    m, ka = o_a.shape
    kb = o_b.shape[1]
    n = w_proj_a.shape[1]
    gb_blk = n // tn
    return pl.pallas_call(
        _merge_kernel,
        grid=(m // tm, n // tn),
        in_specs=[_row_spec(tm, ka),
                  _row_spec(tm, kb),
                  pl.BlockSpec((tm, tn), lambda i, j: (i, j)),
                  pl.BlockSpec((tm, tn), lambda i, j: (i, gb_blk + j)),
                  pl.BlockSpec((2, tn), lambda i, j: (0, j)),
                  pl.BlockSpec((ka, tn), lambda i, j: (0, j)),
                  pl.BlockSpec((kb, tn), lambda i, j: (0, j))],
        out_specs=pl.BlockSpec((tm, tn), lambda i, j: (i, j)),
        out_shape=jax.ShapeDtypeStruct((m, n), BF16),
        compiler_params=_cparams(("arbitrary", "arbitrary")),
        name=name,
    )(o_a, o_b, gates, gates, b_gate, w_proj_a, w_proj_b)


def _swiglu_kernel(h_ref, rs_ref, wg_ref, wu_ref, o_ref):
    wg = wg_ref[...].astype(BF16)
    wu = wu_ref[...].astype(BF16)
    chunk = h_ref.shape[0] // ROW_CHUNKS
    prods = []
    for c in range(ROW_CHUNKS):
        h = h_ref[c * chunk:(c + 1) * chunk, :]
        prods.append((jnp.dot(h, wg, preferred_element_type=F32),
                      jnp.dot(h, wu, preferred_element_type=F32)))
    for c, (g, u) in enumerate(prods):
        rows = slice(c * chunk, (c + 1) * chunk)
        rs = rs_ref[rows, :]
        rs = jnp.concatenate([rs] * (o_ref.shape[1] // rs.shape[1]), axis=1)
        g = g * rs
        o_ref[rows, :] = (g * jax.nn.sigmoid(g) * (u * rs)).astype(o_ref.dtype)


def _swiglu(h, row_scale, w_gate, w_up, *, tm, tn, name):
    m, k = h.shape
    n = w_gate.shape[1]
    return pl.pallas_call(
        _swiglu_kernel,
        grid=(m // tm, n // tn),
        in_specs=[_row_spec(tm, k),
                  pl.BlockSpec((tm, row_scale.shape[1]), lambda i, j: (i, 0)),
                  pl.BlockSpec((k, tn), lambda i, j: (0, j)),
                  pl.BlockSpec((k, tn), lambda i, j: (0, j))],
        out_specs=pl.BlockSpec((tm, tn), lambda i, j: (i, j)),
        out_shape=jax.ShapeDtypeStruct((m, n), BF16),
        compiler_params=_cparams(("arbitrary", "arbitrary")),
        name=name,
    )(h, row_scale, w_gate, w_up)


def _head_rms(x, g):
    ms = jnp.mean(x * x, axis=-1, keepdims=True)
    return x * lax.rsqrt(ms + EPS) * g


def _rope(x, cos, sin_signed):
    lane = lax.broadcasted_iota(jnp.int32, x.shape, 1)
    first_half = (lane % AXIS_ROPE_DIM) < (AXIS_ROPE_DIM // 2)
    partner = jnp.where(first_half,
                        pltpu.roll(x, HEAD_DIM - AXIS_ROPE_DIM // 2, 1),
                        pltpu.roll(x, AXIS_ROPE_DIM // 2, 1))
    return x * cos + partner * sin_signed


def _rope_tables(seq):
    rows = seq // GRID_W
    row = jnp.broadcast_to(jnp.arange(rows)[:, None], (rows, GRID_W)).reshape(-1).astype(F32)
    col = jnp.broadcast_to(jnp.arange(GRID_W)[None, :], (rows, GRID_W)).reshape(-1).astype(F32)
    inv = ROPE_THETA ** (-jnp.arange(0, AXIS_ROPE_DIM, 2, dtype=F32) / AXIS_ROPE_DIM)
    ang_row, ang_col = row[:, None] * inv, col[:, None] * inv
    cr, sr, cc, sc = jnp.cos(ang_row), jnp.sin(ang_row), jnp.cos(ang_col), jnp.sin(ang_col)
    cos = jnp.concatenate([cr, cr, cc, cc], axis=-1)
    sin_signed = jnp.concatenate([-sr, sr, -sc, sc], axis=-1)
    return cos, sin_signed


SOFTMAX_ROWS = 16
LOG2E = math.log2(math.e)


def _attn_a_kernel(q_ref, k_ref, v_ref, cq_ref, sq_ref, ck_ref, sk_ref, gq_ref, gk_ref,
                   o_ref, kn_ref, v1_ref, s_ref, p_ref):
    @pl.when(pl.program_id(2) == 0)
    def _():
        k = _head_rms(k_ref[0].astype(F32), gk_ref[...])
        kn_ref[...] = _rope(k, ck_ref[...], sk_ref[...]).astype(BF16)
        v1_ref[:, :HEAD_DIM] = v_ref[0]
        v1_ref[:, HEAD_DIM:] = jnp.ones((v_ref.shape[1], HEAD_DIM), BF16)

    kn = kn_ref[...]
    v1 = v1_ref[...]
    cq, sq, gq = cq_ref[...], sq_ref[...], gq_ref[...]
    tq = q_ref.shape[1]
    def scores(g):
        q = _head_rms(q_ref[0, :, g * HEAD_DIM:(g + 1) * HEAD_DIM].astype(F32), gq)
        q = (_rope(q, cq, sq) * (ATTN_SCALE * LOG2E)).astype(BF16)
        s_ref[g % 2] = lax.dot_general(q, kn, (((1,), (1,)), ((), ())),
                                       preferred_element_type=F32)

    def softmax(g):
        slot = g % 2
        for r in range(tq // SOFTMAX_ROWS):
            rows = pl.ds(r * SOFTMAX_ROWS, SOFTMAX_ROWS)
            s = s_ref[slot, rows, :]
            p_ref[slot, rows, :] = jnp.exp2(s - jnp.max(s, axis=-1, keepdims=True)).astype(BF16)

    def weighted_values(g):
        ov = jnp.dot(p_ref[g % 2], v1, preferred_element_type=F32)
        o_ref[0, :, g * HEAD_DIM:(g + 1) * HEAD_DIM] = (
            ov[:, :HEAD_DIM] / ov[:, HEAD_DIM:]).astype(o_ref.dtype)

    scores(0)
    for g in range(N_A_GROUP):
        if g + 1 < N_A_GROUP:
            scores(g + 1)
        softmax(g)
        weighted_values(g)


def _attention_a(qkv3, cos, sin_signed, gq, gk, *, tq):
    b, s, _ = qkv3.shape
    kb = N_A_HEADS
    vb = N_A_HEADS + N_A_KV_HEADS
    return pl.pallas_call(
        _attn_a_kernel,
        grid=(b, N_A_KV_HEADS, s // tq),
        in_specs=[pl.BlockSpec((1, tq, HEAD_BLOCK), lambda bi, kv, qi: (bi, qi, kv)),
                  pl.BlockSpec((1, s, HEAD_DIM), lambda bi, kv, qi: (bi, 0, kb + kv)),
                  pl.BlockSpec((1, s, HEAD_DIM), lambda bi, kv, qi: (bi, 0, vb + kv)),
                  pl.BlockSpec((tq, HEAD_DIM), lambda bi, kv, qi: (qi, 0)),
                  pl.BlockSpec((tq, HEAD_DIM), lambda bi, kv, qi: (qi, 0)),
                  pl.BlockSpec((s, HEAD_DIM), lambda bi, kv, qi: (0, 0)),
                  pl.BlockSpec((s, HEAD_DIM), lambda bi, kv, qi: (0, 0)),
                  pl.BlockSpec((1, HEAD_DIM), lambda bi, kv, qi: (0, 0)),
                  pl.BlockSpec((1, HEAD_DIM), lambda bi, kv, qi: (0, 0))],
        out_specs=pl.BlockSpec((1, tq, HEAD_BLOCK), lambda bi, kv, qi: (bi, qi, kv)),
        out_shape=jax.ShapeDtypeStruct((b, s, N_A_HEADS * HEAD_DIM), BF16),
        scratch_shapes=[pltpu.VMEM((s, HEAD_DIM), BF16),
                        pltpu.VMEM((s, 2 * HEAD_DIM), BF16),
                        pltpu.VMEM((2, tq, s), F32),
                        pltpu.VMEM((2, tq, s), BF16)],
        compiler_params=_cparams(("arbitrary", "arbitrary", "arbitrary")),
        name="attention_a",
    )(qkv3, qkv3, qkv3, cos, sin_signed, cos, sin_signed, gq, gk)


RADIUS = 64
assert all(w // (2 * d) == RADIUS for w, d in B_PATTERNS)
SCORES_AHEAD = 1


def _t5_bucket_np(rel):
    nb = REL_BUCKETS // 2
    max_exact = nb // 2
    side = np.where(rel > 0, nb, 0)
    n = np.abs(rel)
    nf = np.maximum(n, 1).astype(np.float32)
    large = max_exact + (np.log(nf / np.float32(max_exact))
                         / np.float32(math.log(REL_MAX_DIST / max_exact))
                         * np.float32(nb - max_exact)).astype(np.int32)
    large = np.minimum(large, nb - 1)
    return (side + np.where(n < max_exact, n, large)).astype(np.int32)


def _band_geometry(seq, dil):
    length = seq // dil
    kw = min(2 * Q_TILE, length)
    nblk = length // Q_TILE
    starts = [min(max(mb * Q_TILE - RADIUS, 0), length - kw) for mb in range(nblk)]
    shifts = sorted({st - mb * Q_TILE for mb, st in enumerate(starts)}, reverse=True)
    return length, kw, nblk, shifts


def _bucket_tiles(seq, dil):
    _, kw, _, shifts = _band_geometry(seq, dil)
    i = np.arange(Q_TILE)[:, None]
    j = np.arange(kw)[None, :]
    tiles = []
    for shift in shifts:
        rel = j - i + shift
        tiles.append(np.where(np.abs(rel) <= RADIUS, _t5_bucket_np(rel * dil), -1))
    return np.stack(tiles, axis=0).astype(np.int32)


def _attn_b_kernel(rb_ref, bk0_ref, bk1_ref, bk2_ref,
                   q0_ref, q1_ref, q2_ref, k0_ref, k1_ref, k2_ref, v0_ref, v1_ref, v2_ref,
                   gq_ref, gk_ref, o_ref,
                   qc_ref, kc_ref, vc_ref, on_ref, ln_ref, t0_ref, t1_ref, t2_ref, *, seq):
    head = pl.program_id(0)
    gq, gk = gq_ref[...], gk_ref[...]
    bk_refs = (bk0_ref, bk1_ref, bk2_ref)
    tile_refs = (t0_ref, t1_ref, t2_ref)
    q_refs, k_refs, v_refs = (q0_ref, q1_ref, q2_ref), (k0_ref, k1_ref, k2_ref), (v0_ref, v1_ref, v2_ref)

    @pl.when(pl.program_id(1) == 0)
    def _():
        for g in range(N_B_GROUPS):
            col = g * N_B_HEADS_PER_GROUP + head
            for var in range(bk_refs[g].shape[0]):
                bkt = bk_refs[g][var]

                def pick(b, acc, bkt=bkt, col=col):
                    return jnp.where(bkt == b, rb_ref[b, col], acc)

                tile_refs[g][var] = lax.fori_loop(0, REL_BUCKETS, pick,
                                                  jnp.full(bkt.shape, NEG_INF, F32))

    for g, (_, dil) in enumerate(B_PATTERNS):
        length, kw, nblk, shifts = _band_geometry(seq, dil)

        for c in range(dil):
            rows = pl.ds(c, length, stride=dil) if dil > 1 else pl.ds(0, length)
            dst = pl.ds(c * length, length)
            qc_ref[g, dst, :] = (_head_rms(q_refs[g][0, rows, :], gq) * ATTN_SCALE).astype(BF16)
            kc_ref[g, dst, :] = _head_rms(k_refs[g][0, rows, :], gk).astype(BF16)
            vc_ref[g, dst, :] = v_refs[g][0, rows, :].astype(BF16)

        def key_rows(c, mb):
            start = min(max(mb * Q_TILE - RADIUS, 0), length - kw)
            return pl.ds(c * length + start, kw), shifts.index(start - mb * Q_TILE)

        def scores(c, mb):
            krows, var = key_rows(c, mb)
            q = qc_ref[g, pl.ds(c * length + mb * Q_TILE, Q_TILE), :]
            logits = lax.dot_general(q, kc_ref[g, krows, :], (((1,), (1,)), ((), ())),
                                     preferred_element_type=F32)
            return logits + tile_refs[g][var]

        def finish(c, mb, logits):
            krows, _ = key_rows(c, mb)
            m = jnp.max(logits, axis=-1, keepdims=True)
            p = jnp.exp(logits - m)
            denom = jnp.sum(p, axis=-1, keepdims=True)
            o = jnp.dot(p.astype(BF16), vc_ref[g, krows, :], preferred_element_type=F32) / denom
            t0 = c + dil * mb * Q_TILE
            rows = pl.ds(t0, Q_TILE, stride=dil) if dil > 1 else pl.ds(t0, Q_TILE)
            on_ref[g, rows, :] = o
            ln_ref[g, rows, :] = jnp.broadcast_to(m + jnp.log(denom), (Q_TILE, HEAD_DIM))

        blocks = [(c, mb) for c in range(dil) for mb in range(nblk)]
        ahead = [scores(*blk) for blk in blocks[:SCORES_AHEAD]]
        for i, blk in enumerate(blocks):
            if i + SCORES_AHEAD < len(blocks):
                ahead.append(scores(*blocks[i + SCORES_AHEAD]))
            finish(*blk, ahead.pop(0))

    chunk = 2 * Q_TILE

    def combine(i, carry):
        rows = pl.ds(pl.multiple_of(i * chunk, chunk), chunk)
        l0, l1, l2 = ln_ref[0, rows, :], ln_ref[1, rows, :], ln_ref[2, rows, :]
        mx = jnp.maximum(jnp.maximum(l0, l1), l2)
        e0, e1, e2 = jnp.exp(l0 - mx), jnp.exp(l1 - mx), jnp.exp(l2 - mx)
        num = e0 * on_ref[0, rows, :] + e1 * on_ref[1, rows, :] + e2 * on_ref[2, rows, :]
        o_ref[0, rows, :] = (num / (e0 + e1 + e2)).astype(o_ref.dtype)
        return carry

    lax.fori_loop(0, seq // chunk, combine, 0)


def _attention_b(qkv3, rel_bias, gq, gk):
    b, s, _ = qkv3.shape
    nh = N_B_HEADS_PER_GROUP
    bks = [jnp.asarray(_bucket_tiles(s, dil)) for _, dil in B_PATTERNS]

    def head_spec(part, g):
        base = (part * N_B_GROUPS + g) * nh
        return pl.BlockSpec((1, s, HEAD_DIM), lambda h, bi: (bi, 0, base + h))

    def full_spec(arr):
        nd = arr.ndim
        return pl.BlockSpec(arr.shape, lambda h, bi: (0,) * nd)

    in_specs = [pl.BlockSpec(memory_space=pltpu.SMEM)]
    in_specs += [full_spec(bk) for bk in bks]
    in_specs += [head_spec(part, g) for part in range(3) for g in range(N_B_GROUPS)]
    in_specs += [pl.BlockSpec((1, HEAD_DIM), lambda h, bi: (0, 0))] * 2
    scratch = [pltpu.VMEM((N_B_GROUPS, s, HEAD_DIM), BF16)] * 3
    scratch += [pltpu.VMEM((N_B_GROUPS, s, HEAD_DIM), F32)] * 2
    scratch += [pltpu.VMEM(bk.shape, F32) for bk in bks]
    return pl.pallas_call(
        functools.partial(_attn_b_kernel, seq=s),
        grid=(nh, b),
        in_specs=in_specs,
        out_specs=pl.BlockSpec((1, s, HEAD_DIM), lambda h, bi: (bi, 0, h)),
        out_shape=jax.ShapeDtypeStruct((b, s, nh * HEAD_DIM), BF16),
        scratch_shapes=scratch,
        compiler_params=_cparams(("arbitrary", "arbitrary")),
        name="attention_b",
    )(rel_bias, *bks, *([qkv3] * 9), gq, gk)


def kernel(x, norm1_g, w_in, b_gate, q_norm_a, k_norm_a, q_norm_b, k_norm_b, rel_bias,
           w_proj_a, w_proj_b, w_out, norm2_g, w_ffn_gate, w_ffn_up, w_ffn_down):
    b, s, d = x.shape
    m = b * s
    a_w = (N_A_HEADS + 2 * N_A_KV_HEADS) * HEAD_DIM
    b_w = 3 * N_B_GROUPS * N_B_HEADS_PER_GROUP * HEAD_DIM
    cos, sin_signed = _rope_tables(s)
    d_ff = w_ffn_gate.shape[-1]
    tk_down = d_ff // 2

    xf = x.reshape(m, d)
    for l in range(norm1_g.shape[0]):
        h = _rmsnorm(xf, norm1_g[l], name="rmsnorm1")
        qkv_a = _matmul(h, w_in[l], col0=0, ncols=a_w, tm=2048, tn=512, out_dtype=BF16,
                        name="in_proj_a")
        qkv_b = _matmul(h, w_in[l], col0=a_w, ncols=b_w, tm=2048, tn=512, out_dtype=F32,
                        name="in_proj_b")
        gates = _matmul(h, w_in[l], col0=a_w + b_w, ncols=2 * d, tm=2048, tn=512, out_dtype=BF16,
                        name="in_proj_gates")

        o_a = _attention_a(qkv_a.reshape(b, s, a_w), cos, sin_signed, q_norm_a[l].reshape(1, -1),
                           k_norm_a[l].reshape(1, -1), tq=256)
        o_b = _attention_b(qkv_b.reshape(b, s, b_w), rel_bias, q_norm_b[l].reshape(1, -1),
                           k_norm_b[l].reshape(1, -1))

        merged = _merge(o_a.reshape(m, -1), o_b.reshape(m, -1), gates, b_gate[l], w_proj_a[l],
                        w_proj_b[l], tm=2048, tn=512, name="gated_merge")
        xf, xg, rs = _matmul_resid_norm(merged, w_out[l], xf, norm2_g[l], tm=2048, tn=256,
                                        rs_width=HEAD_DIM, name="out_proj")
        act = _swiglu(xg, rs, w_ffn_gate[l], w_ffn_up[l], tm=2048, tn=256, name="swiglu_up")
        xf = _matmul_resid_rows(act, w_ffn_down[l], xf, tm=1024, tn=256,
                                vmem_limit=62 * 1024 * 1024, name="ffn_down")
    return xf.reshape(b, s, d)
```

```python
import functools
import math

import jax
import jax.numpy as jnp
import numpy as np
from jax import lax
from jax.experimental import pallas as pl
from jax.experimental.pallas import tpu as pltpu

F32 = jnp.float32
BF16 = jnp.bfloat16

HEAD_DIM = 128
N_A_HEADS = 16
N_A_KV_HEADS = 4
N_A_GROUP = N_A_HEADS // N_A_KV_HEADS
B_PATTERNS = ((128, 1), (512, 4), (2048, 16))
N_B_GROUPS = len(B_PATTERNS)
N_B_HEADS_PER_GROUP = 4
GRID_W = 64
ROPE_THETA = 10000.0
AXIS_ROPE_DIM = HEAD_DIM // 2
REL_BUCKETS = 32
REL_MAX_DIST = 1024
EPS = 1e-6
NEG_INF = -1e30
ATTN_SCALE = HEAD_DIM ** -0.5

Q_TILE = 128
HEAD_BLOCK = N_A_GROUP * HEAD_DIM
VMEM_LIMIT = 62 * 1024 * 1024


def _cparams(sem):
    return pltpu.CompilerParams(dimension_semantics=sem, vmem_limit_bytes=VMEM_LIMIT)


def _rmsnorm_kernel(x_ref, g_ref, o_ref):
    x = x_ref[...]
    ms = jnp.mean(x * x, axis=-1, keepdims=True)
    o_ref[...] = (x * lax.rsqrt(ms + EPS) * g_ref[...]).astype(o_ref.dtype)


def _rmsnorm(x, g, *, tm=512, name):
    m, d = x.shape
    return pl.pallas_call(
        _rmsnorm_kernel,
        grid=(m // tm,),
        in_specs=[pl.BlockSpec((tm, d), lambda i: (i, 0)),
                  pl.BlockSpec((1, d), lambda i: (0, 0))],
        out_specs=pl.BlockSpec((tm, d), lambda i: (i, 0)),
        out_shape=jax.ShapeDtypeStruct((m, d), BF16),
        compiler_params=_cparams(("parallel",)),
        name=name,
    )(x, g.reshape(1, d))


ROW_CHUNKS = 2


def _row_spec(tm, k, single_buffered=False):
    if single_buffered:
        return pl.BlockSpec((tm, k), lambda i, j: (i, 0), pipeline_mode=pl.Buffered(1))
    return pl.BlockSpec((tm, k), lambda i, j: (i, 0))


def _mm_kernel(a_ref, w_ref, o_ref):
    w = w_ref[...].astype(BF16)
    o_ref[...] = jnp.dot(a_ref[...], w, preferred_element_type=F32).astype(o_ref.dtype)


def _matmul(a, w, *, col_ranges, tm, tn, out_dtype, name):
    m, k = a.shape
    ncols = sum(n for _, n in col_ranges)

    def w_block(j):
        blk, first = None, 0
        for c0, n in col_ranges:
            here = c0 // tn + (j - first)
            blk = here if blk is None else jnp.where(j >= first, here, blk)
            first += n // tn
        return blk

    return pl.pallas_call(
        _mm_kernel,
        grid=(m // tm, ncols // tn),
        in_specs=[_row_spec(tm, k),
                  pl.BlockSpec((k, tn), lambda i, j: (0, w_block(j)))],
        out_specs=pl.BlockSpec((tm, tn), lambda i, j: (i, j)),
        out_shape=jax.ShapeDtypeStruct((m, ncols), out_dtype),
        compiler_params=_cparams(("arbitrary", "arbitrary")),
        name=name,
    )(a, w)


def _mm_resid_norm_kernel(a_ref, w_ref, r_ref, g_ref, o_ref, og_ref, rs_ref, ss_ref, *, width):
    j = pl.program_id(1)
    w = w_ref[...].astype(BF16)
    g = g_ref[...]
    tm = a_ref.shape[0]
    chunk = tm // ROW_CHUNKS
    accs = [jnp.dot(a_ref[c * chunk:(c + 1) * chunk, :], w, preferred_element_type=F32)
            for c in range(ROW_CHUNKS)]
    for c, acc in enumerate(accs):
        rows = slice(c * chunk, (c + 1) * chunk)
        y = r_ref[rows, :] + acc
        o_ref[rows, :] = y
        og_ref[rows, :] = (y * g).astype(og_ref.dtype)
        sq = y * y
        part = sq[:, :HEAD_DIM]
        for cc in range(1, sq.shape[1] // HEAD_DIM):
            part = part + sq[:, cc * HEAD_DIM:(cc + 1) * HEAD_DIM]
        ss_ref[rows, :] = jnp.where(j == 0, part, ss_ref[rows, :] + part)

    @pl.when(j == pl.num_programs(1) - 1)
    def _():
        ms = jnp.sum(ss_ref[...], axis=-1, keepdims=True) * (1.0 / width)
        rs_ref[...] = jnp.broadcast_to(lax.rsqrt(ms + EPS), rs_ref.shape)


def _matmul_resid_norm(a, w, resid, gain, *, tm, tn, rs_width, name):
    m, k = a.shape
    n = w.shape[1]
    return pl.pallas_call(
        functools.partial(_mm_resid_norm_kernel, width=n),
        grid=(m // tm, n // tn),
        in_specs=[_row_spec(tm, k, single_buffered=True),
                  pl.BlockSpec((k, tn), lambda i, j: (0, j)),
                  pl.BlockSpec((tm, tn), lambda i, j: (i, j)),
                  pl.BlockSpec((1, tn), lambda i, j: (0, j))],
        out_specs=[pl.BlockSpec((tm, tn), lambda i, j: (i, j)),
                   pl.BlockSpec((tm, tn), lambda i, j: (i, j)),
                   pl.BlockSpec((tm, rs_width), lambda i, j: (i, 0))],
        out_shape=[jax.ShapeDtypeStruct((m, n), F32),
                   jax.ShapeDtypeStruct((m, n), BF16),
                   jax.ShapeDtypeStruct((m, rs_width), F32)],
        scratch_shapes=[pltpu.VMEM((tm, HEAD_DIM), F32)],
        compiler_params=_cparams(("arbitrary", "arbitrary")),
        name=name,
    )(a, w, resid, gain.reshape(1, n))


def _mm_resid_rows_kernel(a_ref, w_ref, r_ref, o_ref):
    w = w_ref[...].astype(BF16)
    o_ref[...] = r_ref[...] + jnp.dot(a_ref[...], w, preferred_element_type=F32)


def _matmul_resid_rows(a, w, resid, *, tm, tn, name):
    m, k = a.shape
    n = w.shape[1]
    return pl.pallas_call(
        _mm_resid_rows_kernel,
        grid=(m // tm, n // tn),
        in_specs=[_row_spec(tm, k, single_buffered=True),
                  pl.BlockSpec((k, tn), lambda i, j: (0, j)),
                  pl.BlockSpec((tm, tn), lambda i, j: (i, j))],
        out_specs=pl.BlockSpec((tm, tn), lambda i, j: (i, j)),
        out_shape=jax.ShapeDtypeStruct((m, n), F32),
        compiler_params=_cparams(("arbitrary", "arbitrary")),
        name=name,
    )(a, w, resid)


def _merge_kernel(oa_ref, ob_ref, ga_ref, gb_ref, bg_ref, wa_ref, wb_ref, o_ref):
    wa = wa_ref[...].astype(BF16)
    wb = wb_ref[...].astype(BF16)
    chunk = oa_ref.shape[0] // ROW_CHUNKS
    prods = []
    for c in range(ROW_CHUNKS):
        rows = slice(c * chunk, (c + 1) * chunk)
        pb = jnp.dot(ob_ref[rows, :], wb, preferred_element_type=F32)
        gate_b = jax.nn.sigmoid(gb_ref[rows, :].astype(F32) + bg_ref[1:2, :])
        pa = jnp.dot(oa_ref[rows, :], wa, preferred_element_type=F32)
        prods.append((pa, gate_b * pb))
    for c, (pa, gpb) in enumerate(prods):
        rows = slice(c * chunk, (c + 1) * chunk)
        gate_a = jax.nn.sigmoid(ga_ref[rows, :].astype(F32) + bg_ref[0:1, :])
        o_ref[rows, :] = (gate_a * pa + gpb).astype(o_ref.dtype)


def _merge(o_a, o_b, gates, b_gate, w_proj_a, w_proj_b, *, gate_col, tm, tn, name):
    m, ka = o_a.shape
    kb = o_b.shape[1]
    n = w_proj_a.shape[1]
    ga_blk = gate_col // tn
    gb_blk = ga_blk + n // tn
    return pl.pallas_call(
        _merge_kernel,
        grid=(m // tm, n // tn),
        in_specs=[_row_spec(tm, ka),
                  _row_spec(tm, kb),
                  pl.BlockSpec((tm, tn), lambda i, j: (i, ga_blk + j)),
                  pl.BlockSpec((tm, tn), lambda i, j: (i, gb_blk + j)),
                  pl.BlockSpec((2, tn), lambda i, j: (0, j)),
                  pl.BlockSpec((ka, tn), lambda i, j: (0, j)),
                  pl.BlockSpec((kb, tn), lambda i, j: (0, j))],
        out_specs=pl.BlockSpec((tm, tn), lambda i, j: (i, j)),
        out_shape=jax.ShapeDtypeStruct((m, n), BF16),
        compiler_params=_cparams(("arbitrary", "arbitrary")),
        name=name,
    )(o_a, o_b, gates, gates, b_gate, w_proj_a, w_proj_b)


def _swiglu_kernel(h_ref, rs_ref, wg_ref, wu_ref, o_ref):
    wg = wg_ref[...].astype(BF16)
    wu = wu_ref[...].astype(BF16)
    chunk = h_ref.shape[0] // ROW_CHUNKS
    prods = []
    for c in range(ROW_CHUNKS):
        h = h_ref[c * chunk:(c + 1) * chunk, :]
        prods.append((jnp.dot(h, wg, preferred_element_type=F32),
                      jnp.dot(h, wu, preferred_element_type=F32)))
    for c, (g, u) in enumerate(prods):
        rows = slice(c * chunk, (c + 1) * chunk)
        rs = rs_ref[rows, :]
        rs = jnp.concatenate([rs] * (o_ref.shape[1] // rs.shape[1]), axis=1)
        g = g * rs
        o_ref[rows, :] = (g * jax.nn.sigmoid(g) * (u * rs)).astype(o_ref.dtype)


def _swiglu(h, row_scale, w_gate, w_up, *, tm, tn, name):
    m, k = h.shape
    n = w_gate.shape[1]
    return pl.pallas_call(
        _swiglu_kernel,
        grid=(m // tm, n // tn),
        in_specs=[_row_spec(tm, k),
                  pl.BlockSpec((tm, row_scale.shape[1]), lambda i, j: (i, 0)),
                  pl.BlockSpec((k, tn), lambda i, j: (0, j)),
                  pl.BlockSpec((k, tn), lambda i, j: (0, j))],
        out_specs=pl.BlockSpec((tm, tn), lambda i, j: (i, j)),
        out_shape=jax.ShapeDtypeStruct((m, n), BF16),
        compiler_params=_cparams(("arbitrary", "arbitrary")),
        name=name,
    )(h, row_scale, w_gate, w_up)


def _head_rms(x, g):
    ms = jnp.mean(x * x, axis=-1, keepdims=True)
    return x * lax.rsqrt(ms + EPS) * g


def _rope(x, cos, sin_signed):
    lane = lax.broadcasted_iota(jnp.int32, x.shape, 1)
    first_half = (lane % AXIS_ROPE_DIM) < (AXIS_ROPE_DIM // 2)
    partner = jnp.where(first_half,
                        pltpu.roll(x, HEAD_DIM - AXIS_ROPE_DIM // 2, 1),
                        pltpu.roll(x, AXIS_ROPE_DIM // 2, 1))
    return x * cos + partner * sin_signed


def _rope_tables(seq):
    t = np.arange(seq)
    row, col = (t // GRID_W).astype(np.float64), (t % GRID_W).astype(np.float64)
    inv = ROPE_THETA ** (-np.arange(0, AXIS_ROPE_DIM, 2, dtype=np.float64) / AXIS_ROPE_DIM)
    ang_row, ang_col = row[:, None] * inv, col[:, None] * inv
    cr, sr, cc, sc = np.cos(ang_row), np.sin(ang_row), np.cos(ang_col), np.sin(ang_col)
    cos = np.concatenate([cr, cr, cc, cc], axis=-1).astype(np.float32)
    sin_signed = np.concatenate([-sr, sr, -sc, sc], axis=-1).astype(np.float32)
    return jnp.asarray(cos), jnp.asarray(sin_signed)


SOFTMAX_ROWS = 16
LOG2E = math.log2(math.e)


def _attn_a_kernel(q_ref, k_ref, v_ref, cq_ref, sq_ref, ck_ref, sk_ref, gq_ref, gk_ref,
                   o_ref, kn_ref, v1_ref, s_ref, p_ref):
    @pl.when(pl.program_id(2) == 0)
    def _():
        k = _head_rms(k_ref[0].astype(F32), gk_ref[...])
        kn_ref[...] = _rope(k, ck_ref[...], sk_ref[...]).astype(BF16)
        v1_ref[:, :HEAD_DIM] = v_ref[0]
        v1_ref[:, HEAD_DIM:] = jnp.ones((v_ref.shape[1], HEAD_DIM), BF16)

    kn = kn_ref[...]
    v1 = v1_ref[...]
    cq, sq, gq = cq_ref[...], sq_ref[...], gq_ref[...]
    tq = q_ref.shape[1]
    def scores(g):
        q = _head_rms(q_ref[0, :, g * HEAD_DIM:(g + 1) * HEAD_DIM].astype(F32), gq)
        q = (_rope(q, cq, sq) * (ATTN_SCALE * LOG2E)).astype(BF16)
        s_ref[g % 2] = lax.dot_general(q, kn, (((1,), (1,)), ((), ())),
                                       preferred_element_type=F32)

    def softmax(g):
        slot = g % 2
        for r in range(tq // SOFTMAX_ROWS):
            rows = pl.ds(r * SOFTMAX_ROWS, SOFTMAX_ROWS)
            s = s_ref[slot, rows, :]
            p_ref[slot, rows, :] = jnp.exp2(s - jnp.max(s, axis=-1, keepdims=True)).astype(BF16)

    def weighted_values(g):
        ov = jnp.dot(p_ref[g % 2], v1, preferred_element_type=F32)
        o_ref[0, :, g * HEAD_DIM:(g + 1) * HEAD_DIM] = (
            ov[:, :HEAD_DIM] / ov[:, HEAD_DIM:]).astype(o_ref.dtype)

    scores(0)
    for g in range(N_A_GROUP):
        if g + 1 < N_A_GROUP:
            scores(g + 1)
        softmax(g)
        weighted_values(g)


def _attention_a(qkv3, cos, sin_signed, gq, gk, *, tq):
    b, s, _ = qkv3.shape
    kb = N_A_HEADS
    vb = N_A_HEADS + N_A_KV_HEADS
    return pl.pallas_call(
        _attn_a_kernel,
        grid=(b, N_A_KV_HEADS, s // tq),
        in_specs=[pl.BlockSpec((1, tq, HEAD_BLOCK), lambda bi, kv, qi: (bi, qi, kv)),
                  pl.BlockSpec((1, s, HEAD_DIM), lambda bi, kv, qi: (bi, 0, kb + kv)),
                  pl.BlockSpec((1, s, HEAD_DIM), lambda bi, kv, qi: (bi, 0, vb + kv)),
                  pl.BlockSpec((tq, HEAD_DIM), lambda bi, kv, qi: (qi, 0)),
                  pl.BlockSpec((tq, HEAD_DIM), lambda bi, kv, qi: (qi, 0)),
                  pl.BlockSpec((s, HEAD_DIM), lambda bi, kv, qi: (0, 0)),
                  pl.BlockSpec((s, HEAD_DIM), lambda bi, kv, qi: (0, 0)),
                  pl.BlockSpec((1, HEAD_DIM), lambda bi, kv, qi: (0, 0)),
                  pl.BlockSpec((1, HEAD_DIM), lambda bi, kv, qi: (0, 0))],
        out_specs=pl.BlockSpec((1, tq, HEAD_BLOCK), lambda bi, kv, qi: (bi, qi, kv)),
        out_shape=jax.ShapeDtypeStruct((b, s, N_A_HEADS * HEAD_DIM), BF16),
        scratch_shapes=[pltpu.VMEM((s, HEAD_DIM), BF16),
                        pltpu.VMEM((s, 2 * HEAD_DIM), BF16),
                        pltpu.VMEM((2, tq, s), F32),
                        pltpu.VMEM((2, tq, s), BF16)],
        compiler_params=_cparams(("arbitrary", "arbitrary", "arbitrary")),
        name="attention_a",
    )(qkv3, qkv3, qkv3, cos, sin_signed, cos, sin_signed, gq, gk)


RADIUS = 64
assert all(w // (2 * d) == RADIUS for w, d in B_PATTERNS)
SCORES_AHEAD = 1


def _t5_bucket_np(rel):
    nb = REL_BUCKETS // 2
    max_exact = nb // 2
    side = np.where(rel > 0, nb, 0)
    n = np.abs(rel)
    nf = np.maximum(n, 1).astype(np.float32)
    large = max_exact + (np.log(nf / np.float32(max_exact))
                         / np.float32(math.log(REL_MAX_DIST / max_exact))
                         * np.float32(nb - max_exact)).astype(np.int32)
    large = np.minimum(large, nb - 1)
    return (side + np.where(n < max_exact, n, large)).astype(np.int32)


def _band_geometry(seq, dil):
    length = seq // dil
    kw = min(2 * Q_TILE, length)
    nblk = length // Q_TILE
    starts = [min(max(mb * Q_TILE - RADIUS, 0), length - kw) for mb in range(nblk)]
    shifts = sorted({st - mb * Q_TILE for mb, st in enumerate(starts)}, reverse=True)
    return length, kw, nblk, shifts


def _bucket_tiles(seq, dil):
    _, kw, _, shifts = _band_geometry(seq, dil)
    i = np.arange(Q_TILE)[:, None]
    j = np.arange(kw)[None, :]
    tiles = []
    for shift in shifts:
        rel = j - i + shift
        tiles.append(np.where(np.abs(rel) <= RADIUS, _t5_bucket_np(rel * dil), -1))
    return np.stack(tiles, axis=0).astype(np.int32)


def _attn_b_kernel(rb_ref, bk0_ref, bk1_ref, bk2_ref,
                   q0_ref, q1_ref, q2_ref, k0_ref, k1_ref, k2_ref, v0_ref, v1_ref, v2_ref,
                   gq_ref, gk_ref, o_ref,
                   qc_ref, kc_ref, vc_ref, on_ref, ln_ref, t0_ref, t1_ref, t2_ref, *, seq):
    head = pl.program_id(0)
    gq, gk = gq_ref[...], gk_ref[...]
    bk_refs = (bk0_ref, bk1_ref, bk2_ref)
    tile_refs = (t0_ref, t1_ref, t2_ref)
    q_refs, k_refs, v_refs = (q0_ref, q1_ref, q2_ref), (k0_ref, k1_ref, k2_ref), (v0_ref, v1_ref, v2_ref)

    @pl.when(pl.program_id(1) == 0)
    def _():
        for g in range(N_B_GROUPS):
            col = g * N_B_HEADS_PER_GROUP + head
            for var in range(bk_refs[g].shape[0]):
                bkt = bk_refs[g][var]

                def pick(b, acc, bkt=bkt, col=col):
                    return jnp.where(bkt == b, rb_ref[b, col], acc)

                tile_refs[g][var] = lax.fori_loop(0, REL_BUCKETS, pick,
                                                  jnp.full(bkt.shape, NEG_INF, F32))

    for g, (_, dil) in enumerate(B_PATTERNS):
        length, kw, nblk, shifts = _band_geometry(seq, dil)

        for c in range(dil):
            rows = pl.ds(c, length, stride=dil) if dil > 1 else pl.ds(0, length)
            dst = pl.ds(c * length, length)
            qc_ref[g, dst, :] = (_head_rms(q_refs[g][0, rows, :], gq) * ATTN_SCALE).astype(BF16)
            kc_ref[g, dst, :] = _head_rms(k_refs[g][0, rows, :], gk).astype(BF16)
            vc_ref[g, dst, :] = v_refs[g][0, rows, :].astype(BF16)

        def key_rows(c, mb):
            start = min(max(mb * Q_TILE - RADIUS, 0), length - kw)
            return pl.ds(c * length + start, kw), shifts.index(start - mb * Q_TILE)

        def scores(c, mb):
            krows, var = key_rows(c, mb)
            q = qc_ref[g, pl.ds(c * length + mb * Q_TILE, Q_TILE), :]
            logits = lax.dot_general(q, kc_ref[g, krows, :], (((1,), (1,)), ((), ())),
                                     preferred_element_type=F32)
            return logits + tile_refs[g][var]

        def finish(c, mb, logits):
            krows, _ = key_rows(c, mb)
            m = jnp.max(logits, axis=-1, keepdims=True)
            p = jnp.exp(logits - m)
            denom = jnp.sum(p, axis=-1, keepdims=True)
            o = jnp.dot(p.astype(BF16), vc_ref[g, krows, :], preferred_element_type=F32) / denom
            t0 = c + dil * mb * Q_TILE
            rows = pl.ds(t0, Q_TILE, stride=dil) if dil > 1 else pl.ds(t0, Q_TILE)
            on_ref[g, rows, :] = o
            ln_ref[g, rows, :] = jnp.broadcast_to(m + jnp.log(denom), (Q_TILE, HEAD_DIM))

        blocks = [(c, mb) for c in range(dil) for mb in range(nblk)]
        ahead = [scores(*blk) for blk in blocks[:SCORES_AHEAD]]
        for i, blk in enumerate(blocks):
            if i + SCORES_AHEAD < len(blocks):
                ahead.append(scores(*blocks[i + SCORES_AHEAD]))
            finish(*blk, ahead.pop(0))

    chunk = 2 * Q_TILE

    def combine(i, carry):
        rows = pl.ds(pl.multiple_of(i * chunk, chunk), chunk)
        l0, l1, l2 = ln_ref[0, rows, :], ln_ref[1, rows, :], ln_ref[2, rows, :]
        mx = jnp.maximum(jnp.maximum(l0, l1), l2)
        e0, e1, e2 = jnp.exp(l0 - mx), jnp.exp(l1 - mx), jnp.exp(l2 - mx)
        num = e0 * on_ref[0, rows, :] + e1 * on_ref[1, rows, :] + e2 * on_ref[2, rows, :]
        o_ref[0, rows, :] = (num / (e0 + e1 + e2)).astype(o_ref.dtype)
        return carry

    lax.fori_loop(0, seq // chunk, combine, 0)


def _attention_b(qkv3, rel_bias, gq, gk):
    b, s, _ = qkv3.shape
    nh = N_B_HEADS_PER_GROUP
    bks = [jnp.asarray(_bucket_tiles(s, dil)) for _, dil in B_PATTERNS]

    def head_spec(part, g):
        base = (part * N_B_GROUPS + g) * nh
        return pl.BlockSpec((1, s, HEAD_DIM), lambda h, bi: (bi, 0, base + h))

    def full_spec(arr):
        nd = arr.ndim
        return pl.BlockSpec(arr.shape, lambda h, bi: (0,) * nd)

    in_specs = [pl.BlockSpec(memory_space=pltpu.SMEM)]
    in_specs += [full_spec(bk) for bk in bks]
    in_specs += [head_spec(part, g) for part in range(3) for g in range(N_B_GROUPS)]
    in_specs += [pl.BlockSpec((1, HEAD_DIM), lambda h, bi: (0, 0))] * 2
    scratch = [pltpu.VMEM((N_B_GROUPS, s, HEAD_DIM), BF16)] * 3
    scratch += [pltpu.VMEM((N_B_GROUPS, s, HEAD_DIM), F32)] * 2
    scratch += [pltpu.VMEM(bk.shape, F32) for bk in bks]
    return pl.pallas_call(
        functools.partial(_attn_b_kernel, seq=s),
        grid=(nh, b),
        in_specs=in_specs,
        out_specs=pl.BlockSpec((1, s, HEAD_DIM), lambda h, bi: (bi, 0, h)),
        out_shape=jax.ShapeDtypeStruct((b, s, nh * HEAD_DIM), BF16),
        scratch_shapes=scratch,
        compiler_params=_cparams(("arbitrary", "arbitrary")),
        name="attention_b",
    )(rel_bias, *bks, *([qkv3] * 9), gq, gk)


def kernel(x, norm1_g, w_in, b_gate, q_norm_a, k_norm_a, q_norm_b, k_norm_b, rel_bias,
           w_proj_a, w_proj_b, w_out, norm2_g, w_ffn_gate, w_ffn_up, w_ffn_down):
    b, s, d = x.shape
    m = b * s
    a_w = (N_A_HEADS + 2 * N_A_KV_HEADS) * HEAD_DIM
    b_w = 3 * N_B_GROUPS * N_B_HEADS_PER_GROUP * HEAD_DIM
    cos, sin_signed = _rope_tables(s)

    xf = x.reshape(m, d)
    for l in range(norm1_g.shape[0]):
        h = _rmsnorm(xf, norm1_g[l], name="rmsnorm1")
        qkv_a_gates = _matmul(h, w_in[l], col_ranges=((0, a_w), (a_w + b_w, 2 * d)), tm=2048,
                              tn=512, out_dtype=BF16, name="in_proj_a_gates")
        qkv_b = _matmul(h, w_in[l], col_ranges=((a_w, b_w),), tm=2048, tn=512, out_dtype=F32,
                        name="in_proj_b")

        o_a = _attention_a(qkv_a_gates.reshape(b, s, -1), cos, sin_signed,
                           q_norm_a[l].reshape(1, -1), k_norm_a[l].reshape(1, -1), tq=256)
        o_b = _attention_b(qkv_b.reshape(b, s, b_w), rel_bias, q_norm_b[l].reshape(1, -1),
                           k_norm_b[l].reshape(1, -1))

        merged = _merge(o_a.reshape(m, -1), o_b.reshape(m, -1), qkv_a_gates, b_gate[l],
                        w_proj_a[l], w_proj_b[l], gate_col=a_w, tm=2048, tn=512,
                        name="gated_merge")
        xf, xg, rs = _matmul_resid_norm(merged, w_out[l], xf, norm2_g[l], tm=2048, tn=512,
                                        rs_width=HEAD_DIM, name="out_proj")
        act = _swiglu(xg, rs, w_ffn_gate[l], w_ffn_up[l], tm=2048, tn=256, name="swiglu_up")
        xf = _matmul_resid_rows(act, w_ffn_down[l], xf, tm=1024, tn=256, name="ffn_down")
    return xf.reshape(b, s, d)
```

```python
import functools
import math

import jax
import jax.numpy as jnp
import numpy as np
from jax import lax
from jax.experimental import pallas as pl
from jax.experimental.pallas import tpu as pltpu

F32 = jnp.float32
BF16 = jnp.bfloat16

HEAD_DIM = 128
N_A_HEADS = 16
N_A_KV_HEADS = 4
N_A_GROUP = N_A_HEADS // N_A_KV_HEADS
B_PATTERNS = ((128, 1), (512, 4), (2048, 16))
N_B_GROUPS = len(B_PATTERNS)
N_B_HEADS_PER_GROUP = 4
GRID_W = 64
ROPE_THETA = 10000.0
AXIS_ROPE_DIM = HEAD_DIM // 2
REL_BUCKETS = 32
REL_MAX_DIST = 1024
EPS = 1e-6
NEG_INF = -1e30
ATTN_SCALE = HEAD_DIM ** -0.5

Q_TILE = 128
HEAD_BLOCK = N_A_GROUP * HEAD_DIM
VMEM_LIMIT = 62 * 1024 * 1024
LANES = 128

TILES = {
    "in_proj": (2048, 512),
    "gated_merge": (2048, 512),
    "out_proj": (2048, 512),
    "swiglu_up": (2048, 256),
    "ffn_down": (1024, 256),
}
ATTN_A_Q_TILE = 256


def _cparams(sem):
    return pltpu.CompilerParams(dimension_semantics=sem, vmem_limit_bytes=VMEM_LIMIT)


def _rmsnorm_kernel(x_ref, g_ref, o_ref):
    x = x_ref[...]
    ms = jnp.mean(x * x, axis=-1, keepdims=True)
    o_ref[...] = (x * lax.rsqrt(ms + EPS) * g_ref[...]).astype(o_ref.dtype)


def _rmsnorm(x, g, *, tm=512, name):
    m, d = x.shape
    return pl.pallas_call(
        _rmsnorm_kernel,
        grid=(m // tm,),
        in_specs=[pl.BlockSpec((tm, d), lambda i: (i, 0)),
                  pl.BlockSpec((1, d), lambda i: (0, 0))],
        out_specs=pl.BlockSpec((tm, d), lambda i: (i, 0)),
        out_shape=jax.ShapeDtypeStruct((m, d), BF16),
        compiler_params=_cparams(("parallel",)),
        name=name,
    )(x, g.reshape(1, d))


ROW_CHUNKS = 2


def _row_spec(tm, k, single_buffered=False):
    if single_buffered:
        return pl.BlockSpec((tm, k), lambda i, j: (i, 0), pipeline_mode=pl.Buffered(1))
    return pl.BlockSpec((tm, k), lambda i, j: (i, 0))


def _mm_kernel(a_ref, w_ref, o_ref):
    w = w_ref[...].astype(BF16)
    o_ref[...] = jnp.dot(a_ref[...], w, preferred_element_type=F32).astype(o_ref.dtype)


def _matmul(a, w, *, col_ranges, tiles, out_dtype, name):
    m, k = a.shape
    tm, tn = tiles
    ncols = sum(n for _, n in col_ranges)
    assert m % tm == 0 and all(c0 % tn == 0 and n % tn == 0 for c0, n in col_ranges)

    def w_block(j):
        blk, first = None, 0
        for c0, n in col_ranges:
            here = c0 // tn + (j - first)
            blk = here if blk is None else jnp.where(j >= first, here, blk)
            first += n // tn
        return blk

    return pl.pallas_call(
        _mm_kernel,
        grid=(m // tm, ncols // tn),
        in_specs=[_row_spec(tm, k),
                  pl.BlockSpec((k, tn), lambda i, j: (0, w_block(j)))],
        out_specs=pl.BlockSpec((tm, tn), lambda i, j: (i, j)),
        out_shape=jax.ShapeDtypeStruct((m, ncols), out_dtype),
        compiler_params=_cparams(("arbitrary", "arbitrary")),
        name=name,
    )(a, w)


def _mm_resid_norm_kernel(a_ref, w_ref, r_ref, g_ref, o_ref, og_ref, rs_ref, ss_ref, *, width):
    j = pl.program_id(1)
    w = w_ref[...].astype(BF16)
    g = g_ref[...]
    tm = a_ref.shape[0]
    chunk = tm // ROW_CHUNKS
    accs = [jnp.dot(a_ref[c * chunk:(c + 1) * chunk, :], w, preferred_element_type=F32)
            for c in range(ROW_CHUNKS)]
    for c, acc in enumerate(accs):
        rows = slice(c * chunk, (c + 1) * chunk)
        y = r_ref[rows, :] + acc
        o_ref[rows, :] = y
        og_ref[rows, :] = (y * g).astype(og_ref.dtype)
        sq = y * y
        part = sq[:, :LANES]
        for cc in range(1, sq.shape[1] // LANES):
            part = part + sq[:, cc * LANES:(cc + 1) * LANES]
        ss_ref[rows, :] = jnp.where(j == 0, part, ss_ref[rows, :] + part)

    @pl.when(j == pl.num_programs(1) - 1)
    def _():
        ms = jnp.sum(ss_ref[...], axis=-1, keepdims=True) * (1.0 / width)
        rs_ref[...] = jnp.broadcast_to(lax.rsqrt(ms + EPS), rs_ref.shape)


def _matmul_resid_norm(a, w, resid, gain, *, tiles, name):
    m, k = a.shape
    n = w.shape[1]
    tm, tn = tiles
    rs_width = LANES
    assert m % tm == 0 and n % tn == 0 and tn % rs_width == 0 and tm % ROW_CHUNKS == 0
    return pl.pallas_call(
        functools.partial(_mm_resid_norm_kernel, width=n),
        grid=(m // tm, n // tn),
        in_specs=[_row_spec(tm, k, single_buffered=True),
                  pl.BlockSpec((k, tn), lambda i, j: (0, j)),
                  pl.BlockSpec((tm, tn), lambda i, j: (i, j)),
                  pl.BlockSpec((1, tn), lambda i, j: (0, j))],
        out_specs=[pl.BlockSpec((tm, tn), lambda i, j: (i, j)),
                   pl.BlockSpec((tm, tn), lambda i, j: (i, j)),
                   pl.BlockSpec((tm, rs_width), lambda i, j: (i, 0))],
        out_shape=[jax.ShapeDtypeStruct((m, n), F32),
                   jax.ShapeDtypeStruct((m, n), BF16),
                   jax.ShapeDtypeStruct((m, rs_width), F32)],
        scratch_shapes=[pltpu.VMEM((tm, LANES), F32)],
        compiler_params=_cparams(("arbitrary", "arbitrary")),
        name=name,
    )(a, w, resid, gain.reshape(1, n))


def _mm_resid_rows_kernel(a_ref, w_ref, r_ref, o_ref):
    w = w_ref[...].astype(BF16)
    o_ref[...] = r_ref[...] + jnp.dot(a_ref[...], w, preferred_element_type=F32)


def _matmul_resid_rows(a, w, resid, *, tiles, name):
    m, k = a.shape
    n = w.shape[1]
    tm, tn = tiles
    assert m % tm == 0 and n % tn == 0
    return pl.pallas_call(
        _mm_resid_rows_kernel,
        grid=(m // tm, n // tn),
        in_specs=[_row_spec(tm, k, single_buffered=True),
                  pl.BlockSpec((k, tn), lambda i, j: (0, j)),
                  pl.BlockSpec((tm, tn), lambda i, j: (i, j))],
        out_specs=pl.BlockSpec((tm, tn), lambda i, j: (i, j)),
        out_shape=jax.ShapeDtypeStruct((m, n), F32),
        compiler_params=_cparams(("arbitrary", "arbitrary")),
        name=name,
    )(a, w, resid)


def _merge_kernel(oa_ref, ob_ref, ga_ref, gb_ref, bg_ref, wa_ref, wb_ref, o_ref):
    wa = wa_ref[...].astype(BF16)
    wb = wb_ref[...].astype(BF16)
    chunk = oa_ref.shape[0] // ROW_CHUNKS
    prods = []
    for c in range(ROW_CHUNKS):
        rows = slice(c * chunk, (c + 1) * chunk)
        pb = jnp.dot(ob_ref[rows, :], wb, preferred_element_type=F32)
        gate_b = jax.nn.sigmoid(gb_ref[rows, :].astype(F32) + bg_ref[1:2, :])
        pa = jnp.dot(oa_ref[rows, :], wa, preferred_element_type=F32)
        prods.append((pa, gate_b * pb))
    for c, (pa, gpb) in enumerate(prods):
        rows = slice(c * chunk, (c + 1) * chunk)
        gate_a = jax.nn.sigmoid(ga_ref[rows, :].astype(F32) + bg_ref[0:1, :])
        o_ref[rows, :] = (gate_a * pa + gpb).astype(o_ref.dtype)


def _merge(o_a, o_b, gates, b_gate, w_proj_a, w_proj_b, *, gate_col, tiles, name):
    m, ka = o_a.shape
    kb = o_b.shape[1]
    n = w_proj_a.shape[1]
    tm, tn = tiles
    assert m % tm == 0 and n % tn == 0 and gate_col % tn == 0 and tm % ROW_CHUNKS == 0
    ga_blk = gate_col // tn
    gb_blk = ga_blk + n // tn
    return pl.pallas_call(
        _merge_kernel,
        grid=(m // tm, n // tn),
        in_specs=[_row_spec(tm, ka),
                  _row_spec(tm, kb),
                  pl.BlockSpec((tm, tn), lambda i, j: (i, ga_blk + j)),
                  pl.BlockSpec((tm, tn), lambda i, j: (i, gb_blk + j)),
                  pl.BlockSpec((2, tn), lambda i, j: (0, j)),
                  pl.BlockSpec((ka, tn), lambda i, j: (0, j)),
                  pl.BlockSpec((kb, tn), lambda i, j: (0, j))],
        out_specs=pl.BlockSpec((tm, tn), lambda i, j: (i, j)),
        out_shape=jax.ShapeDtypeStruct((m, n), BF16),
        compiler_params=_cparams(("arbitrary", "arbitrary")),
        name=name,
    )(o_a, o_b, gates, gates, b_gate, w_proj_a, w_proj_b)


def _swiglu_kernel(h_ref, rs_ref, wg_ref, wu_ref, o_ref):
    wg = wg_ref[...].astype(BF16)
    wu = wu_ref[...].astype(BF16)
    chunk = h_ref.shape[0] // ROW_CHUNKS
    prods = []
    for c in range(ROW_CHUNKS):
        h = h_ref[c * chunk:(c + 1) * chunk, :]
        prods.append((jnp.dot(h, wg, preferred_element_type=F32),
                      jnp.dot(h, wu, preferred_element_type=F32)))
    for c, (g, u) in enumerate(prods):
        rows = slice(c * chunk, (c + 1) * chunk)
        rs = rs_ref[rows, :]
        rs = jnp.concatenate([rs] * (o_ref.shape[1] // rs.shape[1]), axis=1)
        g = g * rs
        o_ref[rows, :] = (g * jax.nn.sigmoid(g) * (u * rs)).astype(o_ref.dtype)


def _swiglu(h, row_scale, w_gate, w_up, *, tiles, name):
    m, k = h.shape
    n = w_gate.shape[1]
    tm, tn = tiles
    assert m % tm == 0 and n % tn == 0 and tn % row_scale.shape[1] == 0 and tm % ROW_CHUNKS == 0
    return pl.pallas_call(
        _swiglu_kernel,
        grid=(m // tm, n // tn),
        in_specs=[_row_spec(tm, k),
                  pl.BlockSpec((tm, row_scale.shape[1]), lambda i, j: (i, 0)),
                  pl.BlockSpec((k, tn), lambda i, j: (0, j)),
                  pl.BlockSpec((k, tn), lambda i, j: (0, j))],
        out_specs=pl.BlockSpec((tm, tn), lambda i, j: (i, j)),
        out_shape=jax.ShapeDtypeStruct((m, n), BF16),
        compiler_params=_cparams(("arbitrary", "arbitrary")),
        name=name,
    )(h, row_scale, w_gate, w_up)


def _head_rms(x, g):
    ms = jnp.mean(x * x, axis=-1, keepdims=True)
    return x * lax.rsqrt(ms + EPS) * g


def _rope(x, cos, sin_signed):
    lane = lax.broadcasted_iota(jnp.int32, x.shape, 1)
    first_half = (lane % AXIS_ROPE_DIM) < (AXIS_ROPE_DIM // 2)
    partner = jnp.where(first_half,
                        pltpu.roll(x, HEAD_DIM - AXIS_ROPE_DIM // 2, 1),
                        pltpu.roll(x, AXIS_ROPE_DIM // 2, 1))
    return x * cos + partner * sin_signed


def _rope_tables(seq):
    t = np.arange(seq)
    row, col = (t // GRID_W).astype(np.float64), (t % GRID_W).astype(np.float64)
    inv = ROPE_THETA ** (-np.arange(0, AXIS_ROPE_DIM, 2, dtype=np.float64) / AXIS_ROPE_DIM)
    ang_row, ang_col = row[:, None] * inv, col[:, None] * inv
    cr, sr, cc, sc = np.cos(ang_row), np.sin(ang_row), np.cos(ang_col), np.sin(ang_col)
    cos = np.concatenate([cr, cr, cc, cc], axis=-1).astype(np.float32)
    sin_signed = np.concatenate([-sr, sr, -sc, sc], axis=-1).astype(np.float32)
    return jnp.asarray(cos), jnp.asarray(sin_signed)


SOFTMAX_ROWS = 16
LOG2E = math.log2(math.e)


def _attn_a_kernel(q_ref, k_ref, v_ref, cq_ref, sq_ref, ck_ref, sk_ref, gq_ref, gk_ref,
                   o_ref, kn_ref, v1_ref, s_ref, p_ref):
    @pl.when(pl.program_id(2) == 0)
    def _():
        k = _head_rms(k_ref[0].astype(F32), gk_ref[...])
        kn_ref[...] = _rope(k, ck_ref[...], sk_ref[...]).astype(BF16)
        v1_ref[:, :HEAD_DIM] = v_ref[0]
        v1_ref[:, HEAD_DIM:] = jnp.ones((v_ref.shape[1], HEAD_DIM), BF16)

    kn = kn_ref[...]
    v1 = v1_ref[...]
    cq, sq, gq = cq_ref[...], sq_ref[...], gq_ref[...]
    tq = q_ref.shape[1]
    def scores(g):
        q = _head_rms(q_ref[0, :, g * HEAD_DIM:(g + 1) * HEAD_DIM].astype(F32), gq)
        q = (_rope(q, cq, sq) * (ATTN_SCALE * LOG2E)).astype(BF16)
        s_ref[g % 2] = lax.dot_general(q, kn, (((1,), (1,)), ((), ())),
                                       preferred_element_type=F32)

    def softmax(g):
        slot = g % 2
        for r in range(tq // SOFTMAX_ROWS):
            rows = pl.ds(r * SOFTMAX_ROWS, SOFTMAX_ROWS)
            s = s_ref[slot, rows, :]
            p_ref[slot, rows, :] = jnp.exp2(s - jnp.max(s, axis=-1, keepdims=True)).astype(BF16)

    def weighted_values(g):
        ov = jnp.dot(p_ref[g % 2], v1, preferred_element_type=F32)
        o_ref[0, :, g * HEAD_DIM:(g + 1) * HEAD_DIM] = (
            ov[:, :HEAD_DIM] / ov[:, HEAD_DIM:]).astype(o_ref.dtype)

    scores(0)
    for g in range(N_A_GROUP):
        if g + 1 < N_A_GROUP:
            scores(g + 1)
        softmax(g)
        weighted_values(g)


def _attention_a(qkv3, cos, sin_signed, gq, gk, *, tq):
    b, s, _ = qkv3.shape
    kb = N_A_HEADS
    vb = N_A_HEADS + N_A_KV_HEADS
    return pl.pallas_call(
        _attn_a_kernel,
        grid=(b, N_A_KV_HEADS, s // tq),
        in_specs=[pl.BlockSpec((1, tq, HEAD_BLOCK), lambda bi, kv, qi: (bi, qi, kv)),
                  pl.BlockSpec((1, s, HEAD_DIM), lambda bi, kv, qi: (bi, 0, kb + kv)),
                  pl.BlockSpec((1, s, HEAD_DIM), lambda bi, kv, qi: (bi, 0, vb + kv)),
                  pl.BlockSpec((tq, HEAD_DIM), lambda bi, kv, qi: (qi, 0)),
                  pl.BlockSpec((tq, HEAD_DIM), lambda bi, kv, qi: (qi, 0)),
                  pl.BlockSpec((s, HEAD_DIM), lambda bi, kv, qi: (0, 0)),
                  pl.BlockSpec((s, HEAD_DIM), lambda bi, kv, qi: (0, 0)),
                  pl.BlockSpec((1, HEAD_DIM), lambda bi, kv, qi: (0, 0)),
                  pl.BlockSpec((1, HEAD_DIM), lambda bi, kv, qi: (0, 0))],
        out_specs=pl.BlockSpec((1, tq, HEAD_BLOCK), lambda bi, kv, qi: (bi, qi, kv)),
        out_shape=jax.ShapeDtypeStruct((b, s, N_A_HEADS * HEAD_DIM), BF16),
        scratch_shapes=[pltpu.VMEM((s, HEAD_DIM), BF16),
                        pltpu.VMEM((s, 2 * HEAD_DIM), BF16),
                        pltpu.VMEM((2, tq, s), F32),
                        pltpu.VMEM((2, tq, s), BF16)],
        compiler_params=_cparams(("arbitrary", "arbitrary", "arbitrary")),
        name="attention_a",
    )(qkv3, qkv3, qkv3, cos, sin_signed, cos, sin_signed, gq, gk)


RADIUS = 64
assert all(w // (2 * d) == RADIUS for w, d in B_PATTERNS)
SCORES_AHEAD = 1


def _t5_bucket_np(rel):
    nb = REL_BUCKETS // 2
    max_exact = nb // 2
    side = np.where(rel > 0, nb, 0)
    n = np.abs(rel)
    nf = np.maximum(n, 1).astype(np.float32)
    large = max_exact + (np.log(nf / np.float32(max_exact))
                         / np.float32(math.log(REL_MAX_DIST / max_exact))
                         * np.float32(nb - max_exact)).astype(np.int32)
    large = np.minimum(large, nb - 1)
    return (side + np.where(n < max_exact, n, large)).astype(np.int32)


def _band_geometry(seq, dil):
    length = seq // dil
    kw = min(2 * Q_TILE, length)
    nblk = length // Q_TILE
    starts = [min(max(mb * Q_TILE - RADIUS, 0), length - kw) for mb in range(nblk)]
    shifts = sorted({st - mb * Q_TILE for mb, st in enumerate(starts)}, reverse=True)
    return length, kw, nblk, shifts


def _bucket_tiles(seq, dil):
    _, kw, _, shifts = _band_geometry(seq, dil)
    i = np.arange(Q_TILE)[:, None]
    j = np.arange(kw)[None, :]
    tiles = []
    for shift in shifts:
        rel = j - i + shift
        tiles.append(np.where(np.abs(rel) <= RADIUS, _t5_bucket_np(rel * dil), -1))
    return np.stack(tiles, axis=0).astype(np.int32)


def _attn_b_kernel(rb_ref, bk0_ref, bk1_ref, bk2_ref,
                   q0_ref, q1_ref, q2_ref, k0_ref, k1_ref, k2_ref, v0_ref, v1_ref, v2_ref,
                   gq_ref, gk_ref, o_ref,
                   qc_ref, kc_ref, vc_ref, on_ref, ln_ref, t0_ref, t1_ref, t2_ref, *, seq):
    head = pl.program_id(0)
    gq, gk = gq_ref[...], gk_ref[...]
    bk_refs = (bk0_ref, bk1_ref, bk2_ref)
    tile_refs = (t0_ref, t1_ref, t2_ref)
    q_refs, k_refs, v_refs = (q0_ref, q1_ref, q2_ref), (k0_ref, k1_ref, k2_ref), (v0_ref, v1_ref, v2_ref)

    @pl.when(pl.program_id(1) == 0)
    def _():
        for g in range(N_B_GROUPS):
            col = g * N_B_HEADS_PER_GROUP + head
            for var in range(bk_refs[g].shape[0]):
                bkt = bk_refs[g][var]

                def pick(b, acc, bkt=bkt, col=col):
                    return jnp.where(bkt == b, rb_ref[b, col], acc)

                tile_refs[g][var] = lax.fori_loop(0, REL_BUCKETS, pick,
                                                  jnp.full(bkt.shape, NEG_INF, F32))

    def prepare(g):
        dil = B_PATTERNS[g][1]
        length = seq // dil
        for c in range(dil):
            rows = pl.ds(c, length, stride=dil) if dil > 1 else pl.ds(0, length)
            dst = pl.ds(c * length, length)
            qc_ref[g, dst, :] = (_head_rms(q_refs[g][0, rows, :], gq) * ATTN_SCALE).astype(BF16)
            kc_ref[g, dst, :] = _head_rms(k_refs[g][0, rows, :], gk).astype(BF16)
            vc_ref[g, dst, :] = v_refs[g][0, rows, :].astype(BF16)

    def run_blocks(g):
        dil = B_PATTERNS[g][1]
        length, kw, nblk, shifts = _band_geometry(seq, dil)

        def key_rows(c, mb):
            start = min(max(mb * Q_TILE - RADIUS, 0), length - kw)
            return pl.ds(c * length + start, kw), shifts.index(start - mb * Q_TILE)

        def scores(c, mb):
            krows, var = key_rows(c, mb)
            q = qc_ref[g, pl.ds(c * length + mb * Q_TILE, Q_TILE), :]
            logits = lax.dot_general(q, kc_ref[g, krows, :], (((1,), (1,)), ((), ())),
                                     preferred_element_type=F32)
            return logits + tile_refs[g][var]

        def finish(c, mb, logits):
            krows, _ = key_rows(c, mb)
            m = jnp.max(logits, axis=-1, keepdims=True)
            p = jnp.exp(logits - m)
            denom = jnp.sum(p, axis=-1, keepdims=True)
            o = jnp.dot(p.astype(BF16), vc_ref[g, krows, :], preferred_element_type=F32) / denom
            t0 = c + dil * mb * Q_TILE
            rows = pl.ds(t0, Q_TILE, stride=dil) if dil > 1 else pl.ds(t0, Q_TILE)
            on_ref[g, rows, :] = o
            ln_ref[g, rows, :] = jnp.broadcast_to(m + jnp.log(denom), (Q_TILE, HEAD_DIM))

        blocks = [(c, mb) for c in range(dil) for mb in range(nblk)]
        ahead = [scores(*blk) for blk in blocks[:SCORES_AHEAD]]
        for i, blk in enumerate(blocks):
            if i + SCORES_AHEAD < len(blocks):
                ahead.append(scores(*blocks[i + SCORES_AHEAD]))
            finish(*blk, ahead.pop(0))

    for g in range(N_B_GROUPS):
        prepare(g)
        run_blocks(g)

    chunk = 2 * Q_TILE

    def combine(i, carry):
        rows = pl.ds(pl.multiple_of(i * chunk, chunk), chunk)
        l0, l1, l2 = ln_ref[0, rows, :], ln_ref[1, rows, :], ln_ref[2, rows, :]
        mx = jnp.maximum(jnp.maximum(l0, l1), l2)
        e0, e1, e2 = jnp.exp(l0 - mx), jnp.exp(l1 - mx), jnp.exp(l2 - mx)
        num = e0 * on_ref[0, rows, :] + e1 * on_ref[1, rows, :] + e2 * on_ref[2, rows, :]
        o_ref[0, rows, :] = (num / (e0 + e1 + e2)).astype(o_ref.dtype)
        return carry

    lax.fori_loop(0, seq // chunk, combine, 0)


def _attention_b(qkv3, rel_bias, gq, gk):
    b, s, _ = qkv3.shape
    nh = N_B_HEADS_PER_GROUP
    bks = [jnp.asarray(_bucket_tiles(s, dil)) for _, dil in B_PATTERNS]

    def head_spec(part, g):
        base = (part * N_B_GROUPS + g) * nh
        return pl.BlockSpec((1, s, HEAD_DIM), lambda h, bi: (bi, 0, base + h))

    def full_spec(arr):
        nd = arr.ndim
        return pl.BlockSpec(arr.shape, lambda h, bi: (0,) * nd)

    in_specs = [pl.BlockSpec(memory_space=pltpu.SMEM)]
    in_specs += [full_spec(bk) for bk in bks]
    in_specs += [head_spec(part, g) for part in range(3) for g in range(N_B_GROUPS)]
    in_specs += [pl.BlockSpec((1, HEAD_DIM), lambda h, bi: (0, 0))] * 2
    scratch = [pltpu.VMEM((N_B_GROUPS, s, HEAD_DIM), BF16)] * 3
    scratch += [pltpu.VMEM((N_B_GROUPS, s, HEAD_DIM), F32)] * 2
    scratch += [pltpu.VMEM(bk.shape, F32) for bk in bks]
    return pl.pallas_call(
        functools.partial(_attn_b_kernel, seq=s),
        grid=(nh, b),
        in_specs=in_specs,
        out_specs=pl.BlockSpec((1, s, HEAD_DIM), lambda h, bi: (bi, 0, h)),
        out_shape=jax.ShapeDtypeStruct((b, s, nh * HEAD_DIM), BF16),
        scratch_shapes=scratch,
        compiler_params=_cparams(("arbitrary", "arbitrary")),
        name="attention_b",
    )(rel_bias, *bks, *([qkv3] * 9), gq, gk)


def kernel(x, norm1_g, w_in, b_gate, q_norm_a, k_norm_a, q_norm_b, k_norm_b, rel_bias,
           w_proj_a, w_proj_b, w_out, norm2_g, w_ffn_gate, w_ffn_up, w_ffn_down):
    b, s, d = x.shape
    m = b * s
    a_w = (N_A_HEADS + 2 * N_A_KV_HEADS) * HEAD_DIM
    b_w = 3 * N_B_GROUPS * N_B_HEADS_PER_GROUP * HEAD_DIM
    cos, sin_signed = _rope_tables(s)

    xf = x.reshape(m, d)
    for l in range(norm1_g.shape[0]):
        h = _rmsnorm(xf, norm1_g[l], name="rmsnorm1")
        qkv_a_gates = _matmul(h, w_in[l], col_ranges=((0, a_w), (a_w + b_w, 2 * d)),
                              tiles=TILES["in_proj"], out_dtype=BF16, name="in_proj_a_gates")
        qkv_b = _matmul(h, w_in[l], col_ranges=((a_w, b_w),), tiles=TILES["in_proj"],
                        out_dtype=F32, name="in_proj_b")

        o_a = _attention_a(qkv_a_gates.reshape(b, s, -1), cos, sin_signed,
                           q_norm_a[l].reshape(1, -1), k_norm_a[l].reshape(1, -1),
                           tq=ATTN_A_Q_TILE)
        o_b = _attention_b(qkv_b.reshape(b, s, b_w), rel_bias, q_norm_b[l].reshape(1, -1),
                           k_norm_b[l].reshape(1, -1))

        merged = _merge(o_a.reshape(m, -1), o_b.reshape(m, -1), qkv_a_gates, b_gate[l],
                        w_proj_a[l], w_proj_b[l], gate_col=a_w, tiles=TILES["gated_merge"],
                        name="gated_merge")
        xf, xg, rs = _matmul_resid_norm(merged, w_out[l], xf, norm2_g[l],
                                        tiles=TILES["out_proj"], name="out_proj")
        act = _swiglu(xg, rs, w_ffn_gate[l], w_ffn_up[l], tiles=TILES["swiglu_up"],
                      name="swiglu_up")
        xf = _matmul_resid_rows(act, w_ffn_down[l], xf, tiles=TILES["ffn_down"], name="ffn_down")
    return xf.reshape(b, s, d)
```

```python
import functools
import math

import jax
import jax.numpy as jnp
import numpy as np
from jax import lax
from jax.experimental import pallas as pl
from jax.experimental.pallas import tpu as pltpu

F32 = jnp.float32
BF16 = jnp.bfloat16

HEAD_DIM = 128
N_A_HEADS = 16
N_A_KV_HEADS = 4
N_A_GROUP = N_A_HEADS // N_A_KV_HEADS
B_PATTERNS = ((128, 1), (512, 4), (2048, 16))
N_B_GROUPS = len(B_PATTERNS)
N_B_HEADS_PER_GROUP = 4
GRID_W = 64
ROPE_THETA = 10000.0
AXIS_ROPE_DIM = HEAD_DIM // 2
REL_BUCKETS = 32
REL_MAX_DIST = 1024
EPS = 1e-6
NEG_INF = -1e30
ATTN_SCALE = HEAD_DIM ** -0.5

Q_TILE = 128
HEAD_BLOCK = N_A_GROUP * HEAD_DIM
VMEM_LIMIT = 62 * 1024 * 1024
LANES = 128

TILES = {
    "in_proj": (2048, 512),
    "gated_merge": (2048, 512),
    "out_proj": (2048, 512),
    "swiglu_up": (2048, 256),
    "ffn_down": (1024, 256),
}
ATTN_A_Q_TILE = 256
ATTN_A_KV_PER_STEP = 4


def _cparams(sem):
    return pltpu.CompilerParams(dimension_semantics=sem, vmem_limit_bytes=VMEM_LIMIT)


def _rmsnorm_kernel(x_ref, g_ref, o_ref):
    x = x_ref[...]
    ms = jnp.mean(x * x, axis=-1, keepdims=True)
    o_ref[...] = (x * lax.rsqrt(ms + EPS) * g_ref[...]).astype(o_ref.dtype)


def _rmsnorm(x, g, *, tm=512, name):
    m, d = x.shape
    return pl.pallas_call(
        _rmsnorm_kernel,
        grid=(m // tm,),
        in_specs=[pl.BlockSpec((tm, d), lambda i: (i, 0)),
                  pl.BlockSpec((1, d), lambda i: (0, 0))],
        out_specs=pl.BlockSpec((tm, d), lambda i: (i, 0)),
        out_shape=jax.ShapeDtypeStruct((m, d), BF16),
        compiler_params=_cparams(("parallel",)),
        name=name,
    )(x, g.reshape(1, d))


ROW_CHUNKS = 2


def _row_spec(tm, k, single_buffered=False):
    if single_buffered:
        return pl.BlockSpec((tm, k), lambda i, j: (i, 0), pipeline_mode=pl.Buffered(1))
    return pl.BlockSpec((tm, k), lambda i, j: (i, 0))


def _mm_kernel(a_ref, w_ref, o_ref):
    w = w_ref[...].astype(BF16)
    o_ref[...] = jnp.dot(a_ref[...], w, preferred_element_type=F32).astype(o_ref.dtype)


def _matmul(a, w, *, col_ranges, tiles, out_dtype, name):
    m, k = a.shape
    tm, tn = tiles
    ncols = sum(n for _, n in col_ranges)
    assert m % tm == 0 and all(c0 % tn == 0 and n % tn == 0 for c0, n in col_ranges)

    def w_block(j):
        blk, first = None, 0
        for c0, n in col_ranges:
            here = c0 // tn + (j - first)
            blk = here if blk is None else jnp.where(j >= first, here, blk)
            first += n // tn
        return blk

    return pl.pallas_call(
        _mm_kernel,
        grid=(m // tm, ncols // tn),
        in_specs=[_row_spec(tm, k),
                  pl.BlockSpec((k, tn), lambda i, j: (0, w_block(j)))],
        out_specs=pl.BlockSpec((tm, tn), lambda i, j: (i, j)),
        out_shape=jax.ShapeDtypeStruct((m, ncols), out_dtype),
        compiler_params=_cparams(("arbitrary", "arbitrary")),
        name=name,
    )(a, w)


def _mm_resid_norm_kernel(a_ref, w_ref, r_ref, g_ref, o_ref, og_ref, rs_ref, ss_ref, *, width):
    j = pl.program_id(1)
    w = w_ref[...].astype(BF16)
    g = g_ref[...]
    tm = a_ref.shape[0]
    chunk = tm // ROW_CHUNKS
    accs = [jnp.dot(a_ref[c * chunk:(c + 1) * chunk, :], w, preferred_element_type=F32)
            for c in range(ROW_CHUNKS)]
    for c, acc in enumerate(accs):
        rows = slice(c * chunk, (c + 1) * chunk)
        y = r_ref[rows, :] + acc
        o_ref[rows, :] = y
        og_ref[rows, :] = (y * g).astype(og_ref.dtype)
        sq = y * y
        part = sq[:, :LANES]
        for cc in range(1, sq.shape[1] // LANES):
            part = part + sq[:, cc * LANES:(cc + 1) * LANES]
        ss_ref[rows, :] = jnp.where(j == 0, part, ss_ref[rows, :] + part)

    @pl.when(j == pl.num_programs(1) - 1)
    def _():
        ms = jnp.sum(ss_ref[...], axis=-1, keepdims=True) * (1.0 / width)
        rs_ref[...] = jnp.broadcast_to(lax.rsqrt(ms + EPS), rs_ref.shape)


def _matmul_resid_norm(a, w, resid, gain, *, tiles, name):
    m, k = a.shape
    n = w.shape[1]
    tm, tn = tiles
    rs_width = LANES
    assert m % tm == 0 and n % tn == 0 and tn % rs_width == 0 and tm % ROW_CHUNKS == 0
    return pl.pallas_call(
        functools.partial(_mm_resid_norm_kernel, width=n),
        grid=(m // tm, n // tn),
        in_specs=[_row_spec(tm, k, single_buffered=True),
                  pl.BlockSpec((k, tn), lambda i, j: (0, j)),
                  pl.BlockSpec((tm, tn), lambda i, j: (i, j)),
                  pl.BlockSpec((1, tn), lambda i, j: (0, j))],
        out_specs=[pl.BlockSpec((tm, tn), lambda i, j: (i, j)),
                   pl.BlockSpec((tm, tn), lambda i, j: (i, j)),
                   pl.BlockSpec((tm, rs_width), lambda i, j: (i, 0))],
        out_shape=[jax.ShapeDtypeStruct((m, n), F32),
                   jax.ShapeDtypeStruct((m, n), BF16),
                   jax.ShapeDtypeStruct((m, rs_width), F32)],
        scratch_shapes=[pltpu.VMEM((tm, LANES), F32)],
        compiler_params=_cparams(("arbitrary", "arbitrary")),
        name=name,
    )(a, w, resid, gain.reshape(1, n))


def _mm_resid_rows_kernel(a_ref, w_ref, r_ref, o_ref):
    w = w_ref[...].astype(BF16)
    o_ref[...] = r_ref[...] + jnp.dot(a_ref[...], w, preferred_element_type=F32)


def _matmul_resid_rows(a, w, resid, *, tiles, name):
    m, k = a.shape
    n = w.shape[1]
    tm, tn = tiles
    assert m % tm == 0 and n % tn == 0
    return pl.pallas_call(
        _mm_resid_rows_kernel,
        grid=(m // tm, n // tn),
        in_specs=[_row_spec(tm, k, single_buffered=True),
                  pl.BlockSpec((k, tn), lambda i, j: (0, j)),
                  pl.BlockSpec((tm, tn), lambda i, j: (i, j))],
        out_specs=pl.BlockSpec((tm, tn), lambda i, j: (i, j)),
        out_shape=jax.ShapeDtypeStruct((m, n), F32),
        compiler_params=_cparams(("arbitrary", "arbitrary")),
        name=name,
    )(a, w, resid)


def _merge_kernel(oa_ref, ob_ref, ga_ref, gb_ref, bg_ref, wa_ref, wb_ref, o_ref):
    wa = wa_ref[...].astype(BF16)
    wb = wb_ref[...].astype(BF16)
    chunk = oa_ref.shape[0] // ROW_CHUNKS
    prods = []
    for c in range(ROW_CHUNKS):
        rows = slice(c * chunk, (c + 1) * chunk)
        pb = jnp.dot(ob_ref[rows, :], wb, preferred_element_type=F32)
        gate_b = jax.nn.sigmoid(gb_ref[rows, :].astype(F32) + bg_ref[1:2, :])
        pa = jnp.dot(oa_ref[rows, :], wa, preferred_element_type=F32)
        prods.append((pa, gate_b * pb))
    for c, (pa, gpb) in enumerate(prods):
        rows = slice(c * chunk, (c + 1) * chunk)
        gate_a = jax.nn.sigmoid(ga_ref[rows, :].astype(F32) + bg_ref[0:1, :])
        o_ref[rows, :] = (gate_a * pa + gpb).astype(o_ref.dtype)


def _merge(o_a, o_b, gates, b_gate, w_proj_a, w_proj_b, *, gate_col, tiles, name):
    m, ka = o_a.shape
    kb = o_b.shape[1]
    n = w_proj_a.shape[1]
    tm, tn = tiles
    assert m % tm == 0 and n % tn == 0 and gate_col % tn == 0 and tm % ROW_CHUNKS == 0
    ga_blk = gate_col // tn
    gb_blk = ga_blk + n // tn
    return pl.pallas_call(
        _merge_kernel,
        grid=(m // tm, n // tn),
        in_specs=[_row_spec(tm, ka),
                  _row_spec(tm, kb),
                  pl.BlockSpec((tm, tn), lambda i, j: (i, ga_blk + j)),
                  pl.BlockSpec((tm, tn), lambda i, j: (i, gb_blk + j)),
                  pl.BlockSpec((2, tn), lambda i, j: (0, j)),
                  pl.BlockSpec((ka, tn), lambda i, j: (0, j)),
                  pl.BlockSpec((kb, tn), lambda i, j: (0, j))],
        out_specs=pl.BlockSpec((tm, tn), lambda i, j: (i, j)),
        out_shape=jax.ShapeDtypeStruct((m, n), BF16),
        compiler_params=_cparams(("arbitrary", "arbitrary")),
        name=name,
    )(o_a, o_b, gates, gates, b_gate, w_proj_a, w_proj_b)


def _swiglu_kernel(h_ref, rs_ref, wg_ref, wu_ref, o_ref):
    wg = wg_ref[...].astype(BF16)
    wu = wu_ref[...].astype(BF16)
    chunk = h_ref.shape[0] // ROW_CHUNKS
    prods = []
    for c in range(ROW_CHUNKS):
        h = h_ref[c * chunk:(c + 1) * chunk, :]
        prods.append((jnp.dot(h, wg, preferred_element_type=F32),
                      jnp.dot(h, wu, preferred_element_type=F32)))
    for c, (g, u) in enumerate(prods):
        rows = slice(c * chunk, (c + 1) * chunk)
        rs = rs_ref[rows, :]
        rs = jnp.concatenate([rs] * (o_ref.shape[1] // rs.shape[1]), axis=1)
        g = g * rs
        o_ref[rows, :] = (g * jax.nn.sigmoid(g) * (u * rs)).astype(o_ref.dtype)


def _swiglu(h, row_scale, w_gate, w_up, *, tiles, name):
    m, k = h.shape
    n = w_gate.shape[1]
    tm, tn = tiles
    assert m % tm == 0 and n % tn == 0 and tn % row_scale.shape[1] == 0 and tm % ROW_CHUNKS == 0
    return pl.pallas_call(
        _swiglu_kernel,
        grid=(m // tm, n // tn),
        in_specs=[_row_spec(tm, k),
                  pl.BlockSpec((tm, row_scale.shape[1]), lambda i, j: (i, 0)),
                  pl.BlockSpec((k, tn), lambda i, j: (0, j)),
                  pl.BlockSpec((k, tn), lambda i, j: (0, j))],
        out_specs=pl.BlockSpec((tm, tn), lambda i, j: (i, j)),
        out_shape=jax.ShapeDtypeStruct((m, n), BF16),
        compiler_params=_cparams(("arbitrary", "arbitrary")),
        name=name,
    )(h, row_scale, w_gate, w_up)


def _head_rms(x, g):
    ms = jnp.mean(x * x, axis=-1, keepdims=True)
    return x * lax.rsqrt(ms + EPS) * g


def _rope(x, cos, sin_signed):
    lane = lax.broadcasted_iota(jnp.int32, x.shape, 1)
    first_half = (lane % AXIS_ROPE_DIM) < (AXIS_ROPE_DIM // 2)
    partner = jnp.where(first_half,
                        pltpu.roll(x, HEAD_DIM - AXIS_ROPE_DIM // 2, 1),
                        pltpu.roll(x, AXIS_ROPE_DIM // 2, 1))
    return x * cos + partner * sin_signed


def _rope_tables(seq):
    t = np.arange(seq)
    row, col = (t // GRID_W).astype(np.float64), (t % GRID_W).astype(np.float64)
    inv = ROPE_THETA ** (-np.arange(0, AXIS_ROPE_DIM, 2, dtype=np.float64) / AXIS_ROPE_DIM)
    ang_row, ang_col = row[:, None] * inv, col[:, None] * inv
    cr, sr, cc, sc = np.cos(ang_row), np.sin(ang_row), np.cos(ang_col), np.sin(ang_col)
    cos = np.concatenate([cr, cr, cc, cc], axis=-1).astype(np.float32)
    sin_signed = np.concatenate([-sr, sr, -sc, sc], axis=-1).astype(np.float32)
    return jnp.asarray(cos), jnp.asarray(sin_signed)


SOFTMAX_ROWS = 16
LOG2E = math.log2(math.e)


def _attn_a_kernel(q_ref, k_ref, v_ref, cq_ref, sq_ref, ck_ref, sk_ref, gq_ref, gk_ref,
                   o_ref, kn_ref, v1_ref, s_ref, p_ref):
    n_kv = kn_ref.shape[0]
    n_heads = n_kv * N_A_GROUP

    @pl.when(pl.program_id(2) == 0)
    def _():
        for kv in range(n_kv):
            cols = slice(kv * HEAD_DIM, (kv + 1) * HEAD_DIM)
            k = _head_rms(k_ref[0, :, cols].astype(F32), gk_ref[...])
            kn_ref[kv] = _rope(k, ck_ref[...], sk_ref[...]).astype(BF16)
            v1_ref[kv, :, :HEAD_DIM] = v_ref[0, :, cols]
            v1_ref[kv, :, HEAD_DIM:] = jnp.ones((v_ref.shape[1], HEAD_DIM), BF16)

    cq, sq, gq = cq_ref[...], sq_ref[...], gq_ref[...]
    tq = q_ref.shape[1]

    def scores(u):
        q = _head_rms(q_ref[0, :, u * HEAD_DIM:(u + 1) * HEAD_DIM].astype(F32), gq)
        q = (_rope(q, cq, sq) * (ATTN_SCALE * LOG2E)).astype(BF16)
        s_ref[u % 2] = lax.dot_general(q, kn_ref[u // N_A_GROUP], (((1,), (1,)), ((), ())),
                                       preferred_element_type=F32)

    def softmax(u):
        slot = u % 2
        for r in range(tq // SOFTMAX_ROWS):
            rows = pl.ds(r * SOFTMAX_ROWS, SOFTMAX_ROWS)
            s = s_ref[slot, rows, :]
            p_ref[slot, rows, :] = jnp.exp2(s - jnp.max(s, axis=-1, keepdims=True)).astype(BF16)

    def weighted_values(u):
        ov = jnp.dot(p_ref[u % 2], v1_ref[u // N_A_GROUP],
                     preferred_element_type=F32)
        o_ref[0, :, u * HEAD_DIM:(u + 1) * HEAD_DIM] = (
            ov[:, :HEAD_DIM] / ov[:, HEAD_DIM:]).astype(o_ref.dtype)

    scores(0)
    for u in range(n_heads):
        if u + 1 < n_heads:
            scores(u + 1)
        softmax(u)
        weighted_values(u)


def _attention_a(qkv3, cos, sin_signed, gq, gk, *, tq):
    b, s, _ = qkv3.shape
    n_kv = ATTN_A_KV_PER_STEP
    qw, kvw = n_kv * HEAD_BLOCK, n_kv * HEAD_DIM
    kb = N_A_HEADS * HEAD_DIM // kvw
    vb = (N_A_HEADS + N_A_KV_HEADS) * HEAD_DIM // kvw
    assert N_A_KV_HEADS % n_kv == 0 and s % tq == 0
    return pl.pallas_call(
        _attn_a_kernel,
        grid=(b, N_A_KV_HEADS // n_kv, s // tq),
        in_specs=[pl.BlockSpec((1, tq, qw), lambda bi, kv, qi: (bi, qi, kv)),
                  pl.BlockSpec((1, s, kvw), lambda bi, kv, qi: (bi, 0, kb + kv)),
                  pl.BlockSpec((1, s, kvw), lambda bi, kv, qi: (bi, 0, vb + kv)),
                  pl.BlockSpec((tq, HEAD_DIM), lambda bi, kv, qi: (qi, 0)),
                  pl.BlockSpec((tq, HEAD_DIM), lambda bi, kv, qi: (qi, 0)),
                  pl.BlockSpec((s, HEAD_DIM), lambda bi, kv, qi: (0, 0)),
                  pl.BlockSpec((s, HEAD_DIM), lambda bi, kv, qi: (0, 0)),
                  pl.BlockSpec((1, HEAD_DIM), lambda bi, kv, qi: (0, 0)),
                  pl.BlockSpec((1, HEAD_DIM), lambda bi, kv, qi: (0, 0))],
        out_specs=pl.BlockSpec((1, tq, qw), lambda bi, kv, qi: (bi, qi, kv)),
        out_shape=jax.ShapeDtypeStruct((b, s, N_A_HEADS * HEAD_DIM), BF16),
        scratch_shapes=[pltpu.VMEM((n_kv, s, HEAD_DIM), BF16),
                        pltpu.VMEM((n_kv, s, 2 * HEAD_DIM), BF16),
                        pltpu.VMEM((2, tq, s), F32),
                        pltpu.VMEM((2, tq, s), BF16)],
        compiler_params=_cparams(("arbitrary", "arbitrary", "arbitrary")),
        name="attention_a",
    )(qkv3, qkv3, qkv3, cos, sin_signed, cos, sin_signed, gq, gk)


RADIUS = 64
assert all(w // (2 * d) == RADIUS for w, d in B_PATTERNS)
SCORES_AHEAD = 1


def _t5_bucket_np(rel):
    nb = REL_BUCKETS // 2
    max_exact = nb // 2
    side = np.where(rel > 0, nb, 0)
    n = np.abs(rel)
    nf = np.maximum(n, 1).astype(np.float32)
    large = max_exact + (np.log(nf / np.float32(max_exact))
                         / np.float32(math.log(REL_MAX_DIST / max_exact))
                         * np.float32(nb - max_exact)).astype(np.int32)
    large = np.minimum(large, nb - 1)
    return (side + np.where(n < max_exact, n, large)).astype(np.int32)


def _band_geometry(seq, dil):
    length = seq // dil
    kw = min(2 * Q_TILE, length)
    nblk = length // Q_TILE
    starts = [min(max(mb * Q_TILE - RADIUS, 0), length - kw) for mb in range(nblk)]
    shifts = sorted({st - mb * Q_TILE for mb, st in enumerate(starts)}, reverse=True)
    return length, kw, nblk, shifts


def _bucket_tiles(seq, dil):
    _, kw, _, shifts = _band_geometry(seq, dil)
    i = np.arange(Q_TILE)[:, None]
    j = np.arange(kw)[None, :]
    tiles = []
    for shift in shifts:
        rel = j - i + shift
        tiles.append(np.where(np.abs(rel) <= RADIUS, _t5_bucket_np(rel * dil), -1))
    return np.stack(tiles, axis=0).astype(np.int32)


def _attn_b_kernel(rb_ref, bk0_ref, bk1_ref, bk2_ref,
                   q0_ref, q1_ref, q2_ref, k0_ref, k1_ref, k2_ref, v0_ref, v1_ref, v2_ref,
                   gq_ref, gk_ref, o_ref,
                   qc_ref, kc_ref, vc_ref, on_ref, ln_ref, t0_ref, t1_ref, t2_ref, *, seq):
    head = pl.program_id(0)
    gq, gk = gq_ref[...], gk_ref[...]
    bk_refs = (bk0_ref, bk1_ref, bk2_ref)
    tile_refs = (t0_ref, t1_ref, t2_ref)
    q_refs, k_refs, v_refs = (q0_ref, q1_ref, q2_ref), (k0_ref, k1_ref, k2_ref), (v0_ref, v1_ref, v2_ref)

    @pl.when(pl.program_id(1) == 0)
    def _():
        for g in range(N_B_GROUPS):
            col = g * N_B_HEADS_PER_GROUP + head
            for var in range(bk_refs[g].shape[0]):
                bkt = bk_refs[g][var]

                def pick(b, acc, bkt=bkt, col=col):
                    return jnp.where(bkt == b, rb_ref[b, col], acc)

                tile_refs[g][var] = lax.fori_loop(0, REL_BUCKETS, pick,
                                                  jnp.full(bkt.shape, NEG_INF, F32))

    def prepare(g):
        dil = B_PATTERNS[g][1]
        length = seq // dil
        for c in range(dil):
            rows = pl.ds(c, length, stride=dil) if dil > 1 else pl.ds(0, length)
            dst = pl.ds(c * length, length)
            qc_ref[g, dst, :] = (_head_rms(q_refs[g][0, rows, :], gq) * ATTN_SCALE).astype(BF16)
            kc_ref[g, dst, :] = _head_rms(k_refs[g][0, rows, :], gk).astype(BF16)
            vc_ref[g, dst, :] = v_refs[g][0, rows, :].astype(BF16)

    def run_blocks(g):
        dil = B_PATTERNS[g][1]
        length, kw, nblk, shifts = _band_geometry(seq, dil)

        def key_rows(c, mb):
            start = min(max(mb * Q_TILE - RADIUS, 0), length - kw)
            return pl.ds(c * length + start, kw), shifts.index(start - mb * Q_TILE)

        def scores(c, mb):
            krows, var = key_rows(c, mb)
            q = qc_ref[g, pl.ds(c * length + mb * Q_TILE, Q_TILE), :]
            logits = lax.dot_general(q, kc_ref[g, krows, :], (((1,), (1,)), ((), ())),
                                     preferred_element_type=F32)
            return logits + tile_refs[g][var]

        def finish(c, mb, logits):
            krows, _ = key_rows(c, mb)
            m = jnp.max(logits, axis=-1, keepdims=True)
            p = jnp.exp(logits - m)
            denom = jnp.sum(p, axis=-1, keepdims=True)
            o = jnp.dot(p.astype(BF16), vc_ref[g, krows, :], preferred_element_type=F32) / denom
            t0 = c + dil * mb * Q_TILE
            rows = pl.ds(t0, Q_TILE, stride=dil) if dil > 1 else pl.ds(t0, Q_TILE)
            on_ref[g, rows, :] = o
            ln_ref[g, rows, :] = jnp.broadcast_to(m + jnp.log(denom), (Q_TILE, HEAD_DIM))

        blocks = [(c, mb) for c in range(dil) for mb in range(nblk)]
        ahead = [scores(*blk) for blk in blocks[:SCORES_AHEAD]]
        for i, blk in enumerate(blocks):
            if i + SCORES_AHEAD < len(blocks):
                ahead.append(scores(*blocks[i + SCORES_AHEAD]))
            finish(*blk, ahead.pop(0))

    for g in range(N_B_GROUPS):
        prepare(g)
        run_blocks(g)

    chunk = 2 * Q_TILE

    def combine(i, carry):
        rows = pl.ds(pl.multiple_of(i * chunk, chunk), chunk)
        l0, l1, l2 = ln_ref[0, rows, :], ln_ref[1, rows, :], ln_ref[2, rows, :]
        mx = jnp.maximum(jnp.maximum(l0, l1), l2)
        e0, e1, e2 = jnp.exp(l0 - mx), jnp.exp(l1 - mx), jnp.exp(l2 - mx)
        num = e0 * on_ref[0, rows, :] + e1 * on_ref[1, rows, :] + e2 * on_ref[2, rows, :]
        o_ref[0, rows, :] = (num / (e0 + e1 + e2)).astype(o_ref.dtype)
        return carry

    lax.fori_loop(0, seq // chunk, combine, 0)


def _attention_b(qkv3, rel_bias, gq, gk):
    b, s, _ = qkv3.shape
    nh = N_B_HEADS_PER_GROUP
    bks = [jnp.asarray(_bucket_tiles(s, dil)) for _, dil in B_PATTERNS]

    def head_spec(part, g):
        base = (part * N_B_GROUPS + g) * nh
        return pl.BlockSpec((1, s, HEAD_DIM), lambda h, bi: (bi, 0, base + h))

    def full_spec(arr):
        nd = arr.ndim
        return pl.BlockSpec(arr.shape, lambda h, bi: (0,) * nd)

    in_specs = [pl.BlockSpec(memory_space=pltpu.SMEM)]
    in_specs += [full_spec(bk) for bk in bks]
    in_specs += [head_spec(part, g) for part in range(3) for g in range(N_B_GROUPS)]
    in_specs += [pl.BlockSpec((1, HEAD_DIM), lambda h, bi: (0, 0))] * 2
    scratch = [pltpu.VMEM((N_B_GROUPS, s, HEAD_DIM), BF16)] * 3
    scratch += [pltpu.VMEM((N_B_GROUPS, s, HEAD_DIM), F32)] * 2
    scratch += [pltpu.VMEM(bk.shape, F32) for bk in bks]
    return pl.pallas_call(
        functools.partial(_attn_b_kernel, seq=s),
        grid=(nh, b),
        in_specs=in_specs,
        out_specs=pl.BlockSpec((1, s, HEAD_DIM), lambda h, bi: (bi, 0, h)),
        out_shape=jax.ShapeDtypeStruct((b, s, nh * HEAD_DIM), BF16),
        scratch_shapes=scratch,
        compiler_params=_cparams(("arbitrary", "arbitrary")),
        name="attention_b",
    )(rel_bias, *bks, *([qkv3] * 9), gq, gk)


def kernel(x, norm1_g, w_in, b_gate, q_norm_a, k_norm_a, q_norm_b, k_norm_b, rel_bias,
           w_proj_a, w_proj_b, w_out, norm2_g, w_ffn_gate, w_ffn_up, w_ffn_down):
    b, s, d = x.shape
    m = b * s
    a_w = (N_A_HEADS + 2 * N_A_KV_HEADS) * HEAD_DIM
    b_w = 3 * N_B_GROUPS * N_B_HEADS_PER_GROUP * HEAD_DIM
    cos, sin_signed = _rope_tables(s)

    xf = x.reshape(m, d)
    for l in range(norm1_g.shape[0]):
        h = _rmsnorm(xf, norm1_g[l], name="rmsnorm1")
        qkv_a_gates = _matmul(h, w_in[l], col_ranges=((0, a_w), (a_w + b_w, 2 * d)),
                              tiles=TILES["in_proj"], out_dtype=BF16, name="in_proj_a_gates")
        qkv_b = _matmul(h, w_in[l], col_ranges=((a_w, b_w),), tiles=TILES["in_proj"],
                        out_dtype=F32, name="in_proj_b")

        o_a = _attention_a(qkv_a_gates.reshape(b, s, -1), cos, sin_signed,
                           q_norm_a[l].reshape(1, -1), k_norm_a[l].reshape(1, -1),
                           tq=ATTN_A_Q_TILE)
        o_b = _attention_b(qkv_b.reshape(b, s, b_w), rel_bias, q_norm_b[l].reshape(1, -1),
                           k_norm_b[l].reshape(1, -1))

        merged = _merge(o_a.reshape(m, -1), o_b.reshape(m, -1), qkv_a_gates, b_gate[l],
                        w_proj_a[l], w_proj_b[l], gate_col=a_w, tiles=TILES["gated_merge"],
                        name="gated_merge")
        xf, xg, rs = _matmul_resid_norm(merged, w_out[l], xf, norm2_g[l],
                                        tiles=TILES["out_proj"], name="out_proj")
        act = _swiglu(xg, rs, w_ffn_gate[l], w_ffn_up[l], tiles=TILES["swiglu_up"],
                      name="swiglu_up")
        xf = _matmul_resid_rows(act, w_ffn_down[l], xf, tiles=TILES["ffn_down"], name="ffn_down")
    return xf.reshape(b, s, d)
```

```python
import functools
import math

import jax
import jax.numpy as jnp
import numpy as np
from jax import lax
from jax.experimental import pallas as pl
from jax.experimental.pallas import tpu as pltpu

F32 = jnp.float32
BF16 = jnp.bfloat16

HEAD_DIM = 128
N_A_HEADS = 16
N_A_KV_HEADS = 4
N_A_GROUP = N_A_HEADS // N_A_KV_HEADS
B_PATTERNS = ((128, 1), (512, 4), (2048, 16))
N_B_GROUPS = len(B_PATTERNS)
N_B_HEADS_PER_GROUP = 4
GRID_W = 64
ROPE_THETA = 10000.0
AXIS_ROPE_DIM = HEAD_DIM // 2
REL_BUCKETS = 32
REL_MAX_DIST = 1024
EPS = 1e-6
NEG_INF = -1e30
ATTN_SCALE = HEAD_DIM ** -0.5

Q_TILE = 128
HEAD_BLOCK = N_A_GROUP * HEAD_DIM
VMEM_LIMIT = 62 * 1024 * 1024
LANES = 128

TILES = {
    "in_proj": (2048, 512),
    "gated_merge": (2048, 512),
    "out_proj": (2048, 512),
    "swiglu_up": (2048, 256),
    "ffn_down": (1024, 256),
}
ATTN_A_Q_TILE = 256
ATTN_A_KV_PER_STEP = 4


def _cparams(sem):
    return pltpu.CompilerParams(dimension_semantics=sem, vmem_limit_bytes=VMEM_LIMIT)


def _rmsnorm_kernel(x_ref, g_ref, o_ref):
    x = x_ref[...]
    ms = jnp.mean(x * x, axis=-1, keepdims=True)
    o_ref[...] = (x * lax.rsqrt(ms + EPS) * g_ref[...]).astype(o_ref.dtype)


def _rmsnorm(x, g, *, tm=512, name):
    m, d = x.shape
    return pl.pallas_call(
        _rmsnorm_kernel,
        grid=(m // tm,),
        in_specs=[pl.BlockSpec((tm, d), lambda i: (i, 0)),
                  pl.BlockSpec((1, d), lambda i: (0, 0))],
        out_specs=pl.BlockSpec((tm, d), lambda i: (i, 0)),
        out_shape=jax.ShapeDtypeStruct((m, d), BF16),
        compiler_params=_cparams(("parallel",)),
        name=name,
    )(x, g.reshape(1, d))


ROW_CHUNKS = 2


def _row_spec(tm, k, single_buffered=False):
    if single_buffered:
        return pl.BlockSpec((tm, k), lambda i, j: (i, 0), pipeline_mode=pl.Buffered(1))
    return pl.BlockSpec((tm, k), lambda i, j: (i, 0))


def _mm_kernel(a_ref, w_ref, o_ref):
    w = w_ref[...].astype(BF16)
    o_ref[...] = jnp.dot(a_ref[...], w, preferred_element_type=F32).astype(o_ref.dtype)


def _matmul(a, w, *, col_ranges, tiles, out_dtype, name):
    m, k = a.shape
    tm, tn = tiles
    ncols = sum(n for _, n in col_ranges)
    assert m % tm == 0 and all(c0 % tn == 0 and n % tn == 0 for c0, n in col_ranges)

    def w_block(j):
        blk, first = None, 0
        for c0, n in col_ranges:
            here = c0 // tn + (j - first)
            blk = here if blk is None else jnp.where(j >= first, here, blk)
            first += n // tn
        return blk

    return pl.pallas_call(
        _mm_kernel,
        grid=(m // tm, ncols // tn),
        in_specs=[_row_spec(tm, k),
                  pl.BlockSpec((k, tn), lambda i, j: (0, w_block(j)))],
        out_specs=pl.BlockSpec((tm, tn), lambda i, j: (i, j)),
        out_shape=jax.ShapeDtypeStruct((m, ncols), out_dtype),
        compiler_params=_cparams(("arbitrary", "arbitrary")),
        name=name,
    )(a, w)


def _mm_resid_norm_kernel(a_ref, w_ref, r_ref, g_ref, o_ref, og_ref, rs_ref, ss_ref, *, width):
    j = pl.program_id(1)
    w = w_ref[...].astype(BF16)
    g = g_ref[...]
    tm = a_ref.shape[0]
    chunk = tm // ROW_CHUNKS
    accs = [jnp.dot(a_ref[c * chunk:(c + 1) * chunk, :], w, preferred_element_type=F32)
            for c in range(ROW_CHUNKS)]
    for c, acc in enumerate(accs):
        rows = slice(c * chunk, (c + 1) * chunk)
        y = r_ref[rows, :] + acc
        o_ref[rows, :] = y
        og_ref[rows, :] = (y * g).astype(og_ref.dtype)
        sq = y * y
        part = sq[:, :LANES]
        for cc in range(1, sq.shape[1] // LANES):
            part = part + sq[:, cc * LANES:(cc + 1) * LANES]
        ss_ref[rows, :] = jnp.where(j == 0, part, ss_ref[rows, :] + part)

    @pl.when(j == pl.num_programs(1) - 1)
    def _():
        ms = jnp.sum(ss_ref[...], axis=-1, keepdims=True) * (1.0 / width)
        rs_ref[...] = jnp.broadcast_to(lax.rsqrt(ms + EPS), rs_ref.shape)


def _matmul_resid_norm(a, w, resid, gain, *, tiles, name):
    m, k = a.shape
    n = w.shape[1]
    tm, tn = tiles
    rs_width = LANES
    assert m % tm == 0 and n % tn == 0 and tn % rs_width == 0 and tm % ROW_CHUNKS == 0
    return pl.pallas_call(
        functools.partial(_mm_resid_norm_kernel, width=n),
        grid=(m // tm, n // tn),
        in_specs=[_row_spec(tm, k, single_buffered=True),
                  pl.BlockSpec((k, tn), lambda i, j: (0, j)),
                  pl.BlockSpec((tm, tn), lambda i, j: (i, j)),
                  pl.BlockSpec((1, tn), lambda i, j: (0, j))],
        out_specs=[pl.BlockSpec((tm, tn), lambda i, j: (i, j)),
                   pl.BlockSpec((tm, tn), lambda i, j: (i, j)),
                   pl.BlockSpec((tm, rs_width), lambda i, j: (i, 0))],
        out_shape=[jax.ShapeDtypeStruct((m, n), F32),
                   jax.ShapeDtypeStruct((m, n), BF16),
                   jax.ShapeDtypeStruct((m, rs_width), F32)],
        scratch_shapes=[pltpu.VMEM((tm, LANES), F32)],
        compiler_params=_cparams(("arbitrary", "arbitrary")),
        name=name,
    )(a, w, resid, gain.reshape(1, n))


def _mm_resid_rows_kernel(a_hbm, w_ref, r_ref, o_ref, a_main, a_spare, sems):
    i, j = pl.program_id(0), pl.program_id(1)
    tm = a_main.shape[0]
    half = tm // 2

    def fetch_lower(tile):
        return pltpu.make_async_copy(a_hbm.at[pl.ds(tile * tm, half), :], a_spare, sems.at[0])

    def fetch_upper(tile):
        return pltpu.make_async_copy(a_hbm.at[pl.ds(tile * tm + half, half), :],
                                     a_main.at[pl.ds(half, half), :], sems.at[1])

    def keep_lower():
        return pltpu.make_async_copy(a_spare, a_main.at[pl.ds(0, half), :], sems.at[2])

    def product(rows):
        return jnp.dot(rows, w_ref[...].astype(BF16), preferred_element_type=F32)

    @pl.when(j == 0)
    def _():
        @pl.when(i == 0)
        def _():
            fetch_lower(0).start()

        fetch_upper(i).start()
        fetch_lower(i).wait()
        keep_lower().start()
        fetch_upper(i).wait()
        keep_lower().wait()

    @pl.when(jnp.logical_and(j == 1, i + 1 < pl.num_programs(0)))
    def _():
        fetch_lower(i + 1).start()

    o_ref[...] = r_ref[...] + product(a_main[...])


def _matmul_resid_rows(a, w, resid, *, tiles, name):
    m, k = a.shape
    n = w.shape[1]
    tm, tn = tiles
    assert m % tm == 0 and n % tn == 0 and n // tn >= 2 and tm % 32 == 0
    return pl.pallas_call(
        _mm_resid_rows_kernel,
        grid=(m // tm, n // tn),
        in_specs=[pl.BlockSpec(memory_space=pl.ANY),
                  pl.BlockSpec((k, tn), lambda i, j: (0, j)),
                  pl.BlockSpec((tm, tn), lambda i, j: (i, j))],
        out_specs=pl.BlockSpec((tm, tn), lambda i, j: (i, j)),
        out_shape=jax.ShapeDtypeStruct((m, n), F32),
        scratch_shapes=[pltpu.VMEM((tm, k), a.dtype),
                        pltpu.VMEM((tm // 2, k), a.dtype),
                        pltpu.SemaphoreType.DMA((3,))],
        compiler_params=_cparams(("arbitrary", "arbitrary")),
        name=name,
    )(a, w, resid)


def _merge_kernel(oa_ref, ob_ref, ga_ref, gb_ref, bg_ref, wa_ref, wb_ref, o_ref):
    wa = wa_ref[...].astype(BF16)
    wb = wb_ref[...].astype(BF16)
    chunk = oa_ref.shape[0] // ROW_CHUNKS
    prods = []
    for c in range(ROW_CHUNKS):
        rows = slice(c * chunk, (c + 1) * chunk)
        pb = jnp.dot(ob_ref[rows, :], wb, preferred_element_type=F32)
        gate_b = jax.nn.sigmoid(gb_ref[rows, :].astype(F32) + bg_ref[1:2, :])
        pa = jnp.dot(oa_ref[rows, :], wa, preferred_element_type=F32)
        prods.append((pa, gate_b * pb))
    for c, (pa, gpb) in enumerate(prods):
        rows = slice(c * chunk, (c + 1) * chunk)
        gate_a = jax.nn.sigmoid(ga_ref[rows, :].astype(F32) + bg_ref[0:1, :])
        o_ref[rows, :] = (gate_a * pa + gpb).astype(o_ref.dtype)


def _merge(o_a, o_b, gates, b_gate, w_proj_a, w_proj_b, *, gate_col, tiles, name):
    m, ka = o_a.shape
    kb = o_b.shape[1]
    n = w_proj_a.shape[1]
    tm, tn = tiles
    assert m % tm == 0 and n % tn == 0 and gate_col % tn == 0 and tm % ROW_CHUNKS == 0
    ga_blk = gate_col // tn
    gb_blk = ga_blk + n // tn
    return pl.pallas_call(
        _merge_kernel,
        grid=(m // tm, n // tn),
        in_specs=[_row_spec(tm, ka),
                  _row_spec(tm, kb),
                  pl.BlockSpec((tm, tn), lambda i, j: (i, ga_blk + j)),
                  pl.BlockSpec((tm, tn), lambda i, j: (i, gb_blk + j)),
                  pl.BlockSpec((2, tn), lambda i, j: (0, j)),
                  pl.BlockSpec((ka, tn), lambda i, j: (0, j)),
                  pl.BlockSpec((kb, tn), lambda i, j: (0, j))],
        out_specs=pl.BlockSpec((tm, tn), lambda i, j: (i, j)),
        out_shape=jax.ShapeDtypeStruct((m, n), BF16),
        compiler_params=_cparams(("arbitrary", "arbitrary")),
        name=name,
    )(o_a, o_b, gates, gates, b_gate, w_proj_a, w_proj_b)


def _swiglu_kernel(h_ref, rs_ref, wg_ref, wu_ref, o_ref):
    wg = wg_ref[...].astype(BF16)
    wu = wu_ref[...].astype(BF16)
    chunk = h_ref.shape[0] // ROW_CHUNKS
    prods = []
    for c in range(ROW_CHUNKS):
        h = h_ref[c * chunk:(c + 1) * chunk, :]
        prods.append((jnp.dot(h, wg, preferred_element_type=F32),
                      jnp.dot(h, wu, preferred_element_type=F32)))
    for c, (g, u) in enumerate(prods):
        rows = slice(c * chunk, (c + 1) * chunk)
        rs = rs_ref[rows, :]
        rs = jnp.concatenate([rs] * (o_ref.shape[1] // rs.shape[1]), axis=1)
        g = g * rs
        o_ref[rows, :] = (g * jax.nn.sigmoid(g) * (u * rs)).astype(o_ref.dtype)


def _swiglu(h, row_scale, w_gate, w_up, *, tiles, name):
    m, k = h.shape
    n = w_gate.shape[1]
    tm, tn = tiles
    assert m % tm == 0 and n % tn == 0 and tn % row_scale.shape[1] == 0 and tm % ROW_CHUNKS == 0
    return pl.pallas_call(
        _swiglu_kernel,
        grid=(m // tm, n // tn),
        in_specs=[_row_spec(tm, k),
                  pl.BlockSpec((tm, row_scale.shape[1]), lambda i, j: (i, 0)),
                  pl.BlockSpec((k, tn), lambda i, j: (0, j)),
                  pl.BlockSpec((k, tn), lambda i, j: (0, j))],
        out_specs=pl.BlockSpec((tm, tn), lambda i, j: (i, j)),
        out_shape=jax.ShapeDtypeStruct((m, n), BF16),
        compiler_params=_cparams(("arbitrary", "arbitrary")),
        name=name,
    )(h, row_scale, w_gate, w_up)


def _head_rms(x, g):
    ms = jnp.mean(x * x, axis=-1, keepdims=True)
    return x * lax.rsqrt(ms + EPS) * g


def _rope(x, cos, sin_signed):
    lane = lax.broadcasted_iota(jnp.int32, x.shape, 1)
    first_half = (lane % AXIS_ROPE_DIM) < (AXIS_ROPE_DIM // 2)
    partner = jnp.where(first_half,
                        pltpu.roll(x, HEAD_DIM - AXIS_ROPE_DIM // 2, 1),
                        pltpu.roll(x, AXIS_ROPE_DIM // 2, 1))
    return x * cos + partner * sin_signed


def _rope_tables(seq):
    t = np.arange(seq)
    row, col = (t // GRID_W).astype(np.float64), (t % GRID_W).astype(np.float64)
    inv = ROPE_THETA ** (-np.arange(0, AXIS_ROPE_DIM, 2, dtype=np.float64) / AXIS_ROPE_DIM)
    ang_row, ang_col = row[:, None] * inv, col[:, None] * inv
    cr, sr, cc, sc = np.cos(ang_row), np.sin(ang_row), np.cos(ang_col), np.sin(ang_col)
    cos = np.concatenate([cr, cr, cc, cc], axis=-1).astype(np.float32)
    sin_signed = np.concatenate([-sr, sr, -sc, sc], axis=-1).astype(np.float32)
    return jnp.asarray(cos), jnp.asarray(sin_signed)


SOFTMAX_ROWS = 16
LOG2E = math.log2(math.e)


def _attn_a_kernel(q_ref, k_ref, v_ref, cq_ref, sq_ref, ck_ref, sk_ref, gq_ref, gk_ref,
                   o_ref, kn_ref, v1_ref, s_ref, p_ref):
    n_kv = kn_ref.shape[0]
    n_heads = n_kv * N_A_GROUP

    @pl.when(pl.program_id(2) == 0)
    def _():
        for kv in range(n_kv):
            cols = slice(kv * HEAD_DIM, (kv + 1) * HEAD_DIM)
            k = _head_rms(k_ref[0, :, cols].astype(F32), gk_ref[...])
            kn_ref[kv] = _rope(k, ck_ref[...], sk_ref[...]).astype(BF16)
            v1_ref[kv, :, :HEAD_DIM] = v_ref[0, :, cols]
            v1_ref[kv, :, HEAD_DIM:] = jnp.ones((v_ref.shape[1], HEAD_DIM), BF16)

    cq, sq, gq = cq_ref[...], sq_ref[...], gq_ref[...]
    tq = q_ref.shape[1]

    def scores(u):
        q = _head_rms(q_ref[0, :, u * HEAD_DIM:(u + 1) * HEAD_DIM].astype(F32), gq)
        q = (_rope(q, cq, sq) * (ATTN_SCALE * LOG2E)).astype(BF16)
        s_ref[u % 2] = lax.dot_general(q, kn_ref[u // N_A_GROUP], (((1,), (1,)), ((), ())),
                                       preferred_element_type=F32)

    def softmax(u):
        slot = u % 2
        for r in range(tq // SOFTMAX_ROWS):
            rows = pl.ds(r * SOFTMAX_ROWS, SOFTMAX_ROWS)
            s = s_ref[slot, rows, :]
            p_ref[slot, rows, :] = jnp.exp2(s - jnp.max(s, axis=-1, keepdims=True)).astype(BF16)

    def weighted_values(u):
        ov = jnp.dot(p_ref[u % 2], v1_ref[u // N_A_GROUP],
                     preferred_element_type=F32)
        o_ref[0, :, u * HEAD_DIM:(u + 1) * HEAD_DIM] = (
            ov[:, :HEAD_DIM] / ov[:, HEAD_DIM:]).astype(o_ref.dtype)

    scores(0)
    for u in range(n_heads):
        if u + 1 < n_heads:
            scores(u + 1)
        softmax(u)
        weighted_values(u)


def _attention_a(qkv3, cos, sin_signed, gq, gk, *, tq):
    b, s, _ = qkv3.shape
    n_kv = ATTN_A_KV_PER_STEP
    qw, kvw = n_kv * HEAD_BLOCK, n_kv * HEAD_DIM
    kb = N_A_HEADS * HEAD_DIM // kvw
    vb = (N_A_HEADS + N_A_KV_HEADS) * HEAD_DIM // kvw
    assert N_A_KV_HEADS % n_kv == 0 and s % tq == 0
    return pl.pallas_call(
        _attn_a_kernel,
        grid=(b, N_A_KV_HEADS // n_kv, s // tq),
        in_specs=[pl.BlockSpec((1, tq, qw), lambda bi, kv, qi: (bi, qi, kv)),
                  pl.BlockSpec((1, s, kvw), lambda bi, kv, qi: (bi, 0, kb + kv)),
                  pl.BlockSpec((1, s, kvw), lambda bi, kv, qi: (bi, 0, vb + kv)),
                  pl.BlockSpec((tq, HEAD_DIM), lambda bi, kv, qi: (qi, 0)),
                  pl.BlockSpec((tq, HEAD_DIM), lambda bi, kv, qi: (qi, 0)),
                  pl.BlockSpec((s, HEAD_DIM), lambda bi, kv, qi: (0, 0)),
                  pl.BlockSpec((s, HEAD_DIM), lambda bi, kv, qi: (0, 0)),
                  pl.BlockSpec((1, HEAD_DIM), lambda bi, kv, qi: (0, 0)),
                  pl.BlockSpec((1, HEAD_DIM), lambda bi, kv, qi: (0, 0))],
        out_specs=pl.BlockSpec((1, tq, qw), lambda bi, kv, qi: (bi, qi, kv)),
        out_shape=jax.ShapeDtypeStruct((b, s, N_A_HEADS * HEAD_DIM), BF16),
        scratch_shapes=[pltpu.VMEM((n_kv, s, HEAD_DIM), BF16),
                        pltpu.VMEM((n_kv, s, 2 * HEAD_DIM), BF16),
                        pltpu.VMEM((2, tq, s), F32),
                        pltpu.VMEM((2, tq, s), BF16)],
        compiler_params=_cparams(("arbitrary", "arbitrary", "arbitrary")),
        name="attention_a",
    )(qkv3, qkv3, qkv3, cos, sin_signed, cos, sin_signed, gq, gk)


RADIUS = 64
assert all(w // (2 * d) == RADIUS for w, d in B_PATTERNS)
SCORES_AHEAD = 1


def _t5_bucket_np(rel):
    nb = REL_BUCKETS // 2
    max_exact = nb // 2
    side = np.where(rel > 0, nb, 0)
    n = np.abs(rel)
    nf = np.maximum(n, 1).astype(np.float32)
    large = max_exact + (np.log(nf / np.float32(max_exact))
                         / np.float32(math.log(REL_MAX_DIST / max_exact))
                         * np.float32(nb - max_exact)).astype(np.int32)
    large = np.minimum(large, nb - 1)
    return (side + np.where(n < max_exact, n, large)).astype(np.int32)


def _band_geometry(seq, dil):
    length = seq // dil
    kw = min(2 * Q_TILE, length)
    nblk = length // Q_TILE
    starts = [min(max(mb * Q_TILE - RADIUS, 0), length - kw) for mb in range(nblk)]
    shifts = sorted({st - mb * Q_TILE for mb, st in enumerate(starts)}, reverse=True)
    return length, kw, nblk, shifts


def _bucket_tiles(seq, dil):
    _, kw, _, shifts = _band_geometry(seq, dil)
    i = np.arange(Q_TILE)[:, None]
    j = np.arange(kw)[None, :]
    tiles = []
    for shift in shifts:
        rel = j - i + shift
        tiles.append(np.where(np.abs(rel) <= RADIUS, _t5_bucket_np(rel * dil), -1))
    return np.stack(tiles, axis=0).astype(np.int32)


def _attn_b_kernel(rb_ref, bk0_ref, bk1_ref, bk2_ref,
                   q0_ref, q1_ref, q2_ref, k0_ref, k1_ref, k2_ref, v0_ref, v1_ref, v2_ref,
                   gq_ref, gk_ref, o_ref,
                   qc_ref, kc_ref, vc_ref, on_ref, ln_ref, t0_ref, t1_ref, t2_ref, *, seq):
    head = pl.program_id(0)
    gq, gk = gq_ref[...], gk_ref[...]
    bk_refs = (bk0_ref, bk1_ref, bk2_ref)
    tile_refs = (t0_ref, t1_ref, t2_ref)
    q_refs, k_refs, v_refs = (q0_ref, q1_ref, q2_ref), (k0_ref, k1_ref, k2_ref), (v0_ref, v1_ref, v2_ref)

    @pl.when(pl.program_id(1) == 0)
    def _():
        for g in range(N_B_GROUPS):
            col = g * N_B_HEADS_PER_GROUP + head
            for var in range(bk_refs[g].shape[0]):
                bkt = bk_refs[g][var]

                def pick(b, acc, bkt=bkt, col=col):
                    return jnp.where(bkt == b, rb_ref[b, col], acc)

                tile_refs[g][var] = lax.fori_loop(0, REL_BUCKETS, pick,
                                                  jnp.full(bkt.shape, NEG_INF, F32))

    def prepare(g):
        dil = B_PATTERNS[g][1]
        length = seq // dil
        for c in range(dil):
            rows = pl.ds(c, length, stride=dil) if dil > 1 else pl.ds(0, length)
            dst = pl.ds(c * length, length)
            qc_ref[g, dst, :] = (_head_rms(q_refs[g][0, rows, :], gq) * ATTN_SCALE).astype(BF16)
            kc_ref[g, dst, :] = _head_rms(k_refs[g][0, rows, :], gk).astype(BF16)
            vc_ref[g, dst, :] = v_refs[g][0, rows, :].astype(BF16)

    def run_blocks(g):
        dil = B_PATTERNS[g][1]
        length, kw, nblk, shifts = _band_geometry(seq, dil)

        def key_rows(c, mb):
            start = min(max(mb * Q_TILE - RADIUS, 0), length - kw)
            return pl.ds(c * length + start, kw), shifts.index(start - mb * Q_TILE)

        def scores(c, mb):
            krows, var = key_rows(c, mb)
            q = qc_ref[g, pl.ds(c * length + mb * Q_TILE, Q_TILE), :]
            logits = lax.dot_general(q, kc_ref[g, krows, :], (((1,), (1,)), ((), ())),
                                     preferred_element_type=F32)
            return logits + tile_refs[g][var]

        def finish(c, mb, logits):
            krows, _ = key_rows(c, mb)
            m = jnp.max(logits, axis=-1, keepdims=True)
            p = jnp.exp(logits - m)
            denom = jnp.sum(p, axis=-1, keepdims=True)
            o = jnp.dot(p.astype(BF16), vc_ref[g, krows, :], preferred_element_type=F32) / denom
            t0 = c + dil * mb * Q_TILE
            rows = pl.ds(t0, Q_TILE, stride=dil) if dil > 1 else pl.ds(t0, Q_TILE)
            on_ref[g, rows, :] = o
            ln_ref[g, rows, :] = jnp.broadcast_to(m + jnp.log(denom), (Q_TILE, HEAD_DIM))

        blocks = [(c, mb) for c in range(dil) for mb in range(nblk)]
        ahead = [scores(*blk) for blk in blocks[:SCORES_AHEAD]]
        for i, blk in enumerate(blocks):
            if i + SCORES_AHEAD < len(blocks):
                ahead.append(scores(*blocks[i + SCORES_AHEAD]))
            finish(*blk, ahead.pop(0))

    for g in range(N_B_GROUPS):
        prepare(g)
        run_blocks(g)

    chunk = 2 * Q_TILE

    def combine(i, carry):
        rows = pl.ds(pl.multiple_of(i * chunk, chunk), chunk)
        l0, l1, l2 = ln_ref[0, rows, :], ln_ref[1, rows, :], ln_ref[2, rows, :]
        mx = jnp.maximum(jnp.maximum(l0, l1), l2)
        e0, e1, e2 = jnp.exp(l0 - mx), jnp.exp(l1 - mx), jnp.exp(l2 - mx)
        num = e0 * on_ref[0, rows, :] + e1 * on_ref[1, rows, :] + e2 * on_ref[2, rows, :]
        o_ref[0, rows, :] = (num / (e0 + e1 + e2)).astype(o_ref.dtype)
        return carry

    lax.fori_loop(0, seq // chunk, combine, 0)


def _attention_b(qkv3, rel_bias, gq, gk):
    b, s, _ = qkv3.shape
    nh = N_B_HEADS_PER_GROUP
    bks = [jnp.asarray(_bucket_tiles(s, dil)) for _, dil in B_PATTERNS]

    def head_spec(part, g):
        base = (part * N_B_GROUPS + g) * nh
        return pl.BlockSpec((1, s, HEAD_DIM), lambda h, bi: (bi, 0, base + h))

    def full_spec(arr):
        nd = arr.ndim
        return pl.BlockSpec(arr.shape, lambda h, bi: (0,) * nd)

    in_specs = [pl.BlockSpec(memory_space=pltpu.SMEM)]
    in_specs += [full_spec(bk) for bk in bks]
    in_specs += [head_spec(part, g) for part in range(3) for g in range(N_B_GROUPS)]
    in_specs += [pl.BlockSpec((1, HEAD_DIM), lambda h, bi: (0, 0))] * 2
    scratch = [pltpu.VMEM((N_B_GROUPS, s, HEAD_DIM), BF16)] * 3
    scratch += [pltpu.VMEM((N_B_GROUPS, s, HEAD_DIM), F32)] * 2
    scratch += [pltpu.VMEM(bk.shape, F32) for bk in bks]
    return pl.pallas_call(
        functools.partial(_attn_b_kernel, seq=s),
        grid=(nh, b),
        in_specs=in_specs,
        out_specs=pl.BlockSpec((1, s, HEAD_DIM), lambda h, bi: (bi, 0, h)),
        out_shape=jax.ShapeDtypeStruct((b, s, nh * HEAD_DIM), BF16),
        scratch_shapes=scratch,
        compiler_params=_cparams(("arbitrary", "arbitrary")),
        name="attention_b",
    )(rel_bias, *bks, *([qkv3] * 9), gq, gk)


def kernel(x, norm1_g, w_in, b_gate, q_norm_a, k_norm_a, q_norm_b, k_norm_b, rel_bias,
           w_proj_a, w_proj_b, w_out, norm2_g, w_ffn_gate, w_ffn_up, w_ffn_down):
    b, s, d = x.shape
    m = b * s
    a_w = (N_A_HEADS + 2 * N_A_KV_HEADS) * HEAD_DIM
    b_w = 3 * N_B_GROUPS * N_B_HEADS_PER_GROUP * HEAD_DIM
    cos, sin_signed = _rope_tables(s)

    xf = x.reshape(m, d)
    for l in range(norm1_g.shape[0]):
        h = _rmsnorm(xf, norm1_g[l], name="rmsnorm1")
        qkv_a_gates = _matmul(h, w_in[l], col_ranges=((0, a_w), (a_w + b_w, 2 * d)),
                              tiles=TILES["in_proj"], out_dtype=BF16, name="in_proj_a_gates")
        qkv_b = _matmul(h, w_in[l], col_ranges=((a_w, b_w),), tiles=TILES["in_proj"],
                        out_dtype=F32, name="in_proj_b")

        o_a = _attention_a(qkv_a_gates.reshape(b, s, -1), cos, sin_signed,
                           q_norm_a[l].reshape(1, -1), k_norm_a[l].reshape(1, -1),
                           tq=ATTN_A_Q_TILE)
        o_b = _attention_b(qkv_b.reshape(b, s, b_w), rel_bias, q_norm_b[l].reshape(1, -1),
                           k_norm_b[l].reshape(1, -1))

        merged = _merge(o_a.reshape(m, -1), o_b.reshape(m, -1), qkv_a_gates, b_gate[l],
                        w_proj_a[l], w_proj_b[l], gate_col=a_w, tiles=TILES["gated_merge"],
                        name="gated_merge")
        xf, xg, rs = _matmul_resid_norm(merged, w_out[l], xf, norm2_g[l],
                                        tiles=TILES["out_proj"], name="out_proj")
        act = _swiglu(xg, rs, w_ffn_gate[l], w_ffn_up[l], tiles=TILES["swiglu_up"],
                      name="swiglu_up")
        xf = _matmul_resid_rows(act, w_ffn_down[l], xf, tiles=TILES["ffn_down"], name="ffn_down")
    return xf.reshape(b, s, d)
```

```python
import functools
import math

import jax
import jax.numpy as jnp
import numpy as np
from jax import lax
from jax.experimental import pallas as pl
from jax.experimental.pallas import tpu as pltpu

F32 = jnp.float32
BF16 = jnp.bfloat16

HEAD_DIM = 128
N_A_HEADS = 16
N_A_KV_HEADS = 4
N_A_GROUP = N_A_HEADS // N_A_KV_HEADS
B_PATTERNS = ((128, 1), (512, 4), (2048, 16))
N_B_GROUPS = len(B_PATTERNS)
N_B_HEADS_PER_GROUP = 4
GRID_W = 64
ROPE_THETA = 10000.0
AXIS_ROPE_DIM = HEAD_DIM // 2
REL_BUCKETS = 32
REL_MAX_DIST = 1024
EPS = 1e-6
NEG_INF = -1e30
ATTN_SCALE = HEAD_DIM ** -0.5

Q_TILE = 128
HEAD_BLOCK = N_A_GROUP * HEAD_DIM
VMEM_LIMIT = 62 * 1024 * 1024
LANES = 128

TILES = {
    "in_proj": (2048, 512),
    "gated_merge": (2048, 512),
    "out_proj": (2048, 512),
    "swiglu_up": (2048, 256),
    "ffn_down": (1024, 256),
}
ATTN_A_Q_TILE = 256
ATTN_A_KV_PER_STEP = 4


def _cparams(sem):
    return pltpu.CompilerParams(dimension_semantics=sem, vmem_limit_bytes=VMEM_LIMIT)


def _rmsnorm_kernel(x_ref, g_ref, o_ref):
    x = x_ref[...]
    ms = jnp.mean(x * x, axis=-1, keepdims=True)
    o_ref[...] = (x * lax.rsqrt(ms + EPS) * g_ref[...]).astype(o_ref.dtype)


def _rmsnorm(x, g, *, tm=512, name):
    m, d = x.shape
    return pl.pallas_call(
        _rmsnorm_kernel,
        grid=(m // tm,),
        in_specs=[pl.BlockSpec((tm, d), lambda i: (i, 0)),
                  pl.BlockSpec((1, d), lambda i: (0, 0))],
        out_specs=pl.BlockSpec((tm, d), lambda i: (i, 0)),
        out_shape=jax.ShapeDtypeStruct((m, d), BF16),
        compiler_params=_cparams(("parallel",)),
        name=name,
    )(x, g.reshape(1, d))


ROW_CHUNKS = 2


def _row_spec(tm, k, single_buffered=False):
    if single_buffered:
        return pl.BlockSpec((tm, k), lambda i, j: (i, 0), pipeline_mode=pl.Buffered(1))
    return pl.BlockSpec((tm, k), lambda i, j: (i, 0))


def _mm_kernel(a_ref, w_ref, o_ref):
    w = w_ref[...].astype(BF16)
    o_ref[...] = jnp.dot(a_ref[...], w, preferred_element_type=F32).astype(o_ref.dtype)


def _matmul(a, w, *, col_ranges, tiles, out_dtype, name):
    m, k = a.shape
    tm, tn = tiles
    ncols = sum(n for _, n in col_ranges)
    assert m % tm == 0 and all(c0 % tn == 0 and n % tn == 0 for c0, n in col_ranges)

    def w_block(j):
        blk, first = None, 0
        for c0, n in col_ranges:
            here = c0 // tn + (j - first)
            blk = here if blk is None else jnp.where(j >= first, here, blk)
            first += n // tn
        return blk

    return pl.pallas_call(
        _mm_kernel,
        grid=(m // tm, ncols // tn),
        in_specs=[_row_spec(tm, k),
                  pl.BlockSpec((k, tn), lambda i, j: (0, w_block(j)))],
        out_specs=pl.BlockSpec((tm, tn), lambda i, j: (i, j)),
        out_shape=jax.ShapeDtypeStruct((m, ncols), out_dtype),
        compiler_params=_cparams(("arbitrary", "arbitrary")),
        name=name,
    )(a, w)


def _mm_resid_norm_kernel(a_ref, w_ref, r_ref, g_ref, o_ref, og_ref, rs_ref, ss_ref, *, width):
    j = pl.program_id(1)
    w = w_ref[...].astype(BF16)
    g = g_ref[...]
    tm = a_ref.shape[0]
    chunk = tm // ROW_CHUNKS
    accs = [jnp.dot(a_ref[c * chunk:(c + 1) * chunk, :], w, preferred_element_type=F32)
            for c in range(ROW_CHUNKS)]
    for c, acc in enumerate(accs):
        rows = slice(c * chunk, (c + 1) * chunk)
        y = r_ref[rows, :] + acc
        o_ref[rows, :] = y
        og_ref[rows, :] = (y * g).astype(og_ref.dtype)
        sq = y * y
        part = sq[:, :LANES]
        for cc in range(1, sq.shape[1] // LANES):
            part = part + sq[:, cc * LANES:(cc + 1) * LANES]
        ss_ref[rows, :] = jnp.where(j == 0, part, ss_ref[rows, :] + part)

    @pl.when(j == pl.num_programs(1) - 1)
    def _():
        ms = jnp.sum(ss_ref[...], axis=-1, keepdims=True) * (1.0 / width)
        rs_ref[...] = jnp.broadcast_to(lax.rsqrt(ms + EPS), rs_ref.shape)


def _matmul_resid_norm(a, w, resid, gain, *, tiles, name):
    m, k = a.shape
    n = w.shape[1]
    tm, tn = tiles
    rs_width = LANES
    assert m % tm == 0 and n % tn == 0 and tn % rs_width == 0 and tm % ROW_CHUNKS == 0
    return pl.pallas_call(
        functools.partial(_mm_resid_norm_kernel, width=n),
        grid=(m // tm, n // tn),
        in_specs=[_row_spec(tm, k, single_buffered=True),
                  pl.BlockSpec((k, tn), lambda i, j: (0, j)),
                  pl.BlockSpec((tm, tn), lambda i, j: (i, j)),
                  pl.BlockSpec((1, tn), lambda i, j: (0, j))],
        out_specs=[pl.BlockSpec((tm, tn), lambda i, j: (i, j)),
                   pl.BlockSpec((tm, tn), lambda i, j: (i, j)),
                   pl.BlockSpec((tm, rs_width), lambda i, j: (i, 0))],
        out_shape=[jax.ShapeDtypeStruct((m, n), F32),
                   jax.ShapeDtypeStruct((m, n), BF16),
                   jax.ShapeDtypeStruct((m, rs_width), F32)],
        scratch_shapes=[pltpu.VMEM((tm, LANES), F32)],
        compiler_params=_cparams(("arbitrary", "arbitrary")),
        name=name,
    )(a, w, resid, gain.reshape(1, n))


def _mm_resid_rows_kernel(a_ref, w_ref, r_ref, o_ref):
    w = w_ref[...].astype(BF16)
    o_ref[...] = r_ref[...] + jnp.dot(a_ref[...], w, preferred_element_type=F32)


def _matmul_resid_rows(a, w, resid, *, tiles, name):
    m, k = a.shape
    n = w.shape[1]
    tm, tn = tiles
    assert m % tm == 0 and n % tn == 0
    return pl.pallas_call(
        _mm_resid_rows_kernel,
        grid=(m // tm, n // tn),
        in_specs=[_row_spec(tm, k, single_buffered=True),
                  pl.BlockSpec((k, tn), lambda i, j: (0, j)),
                  pl.BlockSpec((tm, tn), lambda i, j: (i, j))],
        out_specs=pl.BlockSpec((tm, tn), lambda i, j: (i, j)),
        out_shape=jax.ShapeDtypeStruct((m, n), F32),
        compiler_params=_cparams(("arbitrary", "arbitrary")),
        name=name,
    )(a, w, resid)


def _merge_kernel(oa_ref, ob_ref, ga_ref, gb_ref, bg_ref, wa_ref, wb_ref, o_ref):
    wa = wa_ref[...].astype(BF16)
    wb = wb_ref[...].astype(BF16)
    chunk = oa_ref.shape[0] // ROW_CHUNKS
    prods = []
    for c in range(ROW_CHUNKS):
        rows = slice(c * chunk, (c + 1) * chunk)
        pb = jnp.dot(ob_ref[rows, :], wb, preferred_element_type=F32)
        gate_b = jax.nn.sigmoid(gb_ref[rows, :].astype(F32) + bg_ref[1:2, :])
        pa = jnp.dot(oa_ref[rows, :], wa, preferred_element_type=F32)
        prods.append((pa, gate_b * pb))
    for c, (pa, gpb) in enumerate(prods):
        rows = slice(c * chunk, (c + 1) * chunk)
        gate_a = jax.nn.sigmoid(ga_ref[rows, :].astype(F32) + bg_ref[0:1, :])
        o_ref[rows, :] = (gate_a * pa + gpb).astype(o_ref.dtype)


def _merge(o_a, o_b, gates, b_gate, w_proj_a, w_proj_b, *, gate_col, tiles, name):
    m, ka = o_a.shape
    kb = o_b.shape[1]
    n = w_proj_a.shape[1]
    tm, tn = tiles
    assert m % tm == 0 and n % tn == 0 and gate_col % tn == 0 and tm % ROW_CHUNKS == 0
    ga_blk = gate_col // tn
    gb_blk = ga_blk + n // tn
    return pl.pallas_call(
        _merge_kernel,
        grid=(m // tm, n // tn),
        in_specs=[_row_spec(tm, ka),
                  _row_spec(tm, kb),
                  pl.BlockSpec((tm, tn), lambda i, j: (i, ga_blk + j)),
                  pl.BlockSpec((tm, tn), lambda i, j: (i, gb_blk + j)),
                  pl.BlockSpec((2, tn), lambda i, j: (0, j)),
                  pl.BlockSpec((ka, tn), lambda i, j: (0, j)),
                  pl.BlockSpec((kb, tn), lambda i, j: (0, j))],
        out_specs=pl.BlockSpec((tm, tn), lambda i, j: (i, j)),
        out_shape=jax.ShapeDtypeStruct((m, n), BF16),
        compiler_params=_cparams(("arbitrary", "arbitrary")),
        name=name,
    )(o_a, o_b, gates, gates, b_gate, w_proj_a, w_proj_b)


def _swiglu_kernel(h_ref, rs_ref, wg_ref, wu_ref, o_ref):
    wg = wg_ref[...].astype(BF16)
    wu = wu_ref[...].astype(BF16)
    chunk = h_ref.shape[0] // ROW_CHUNKS
    prods = []
    for c in range(ROW_CHUNKS):
        h = h_ref[c * chunk:(c + 1) * chunk, :]
        prods.append((jnp.dot(h, wg, preferred_element_type=F32),
                      jnp.dot(h, wu, preferred_element_type=F32)))
    for c, (g, u) in enumerate(prods):
        rows = slice(c * chunk, (c + 1) * chunk)
        rs = rs_ref[rows, :]
        rs = jnp.concatenate([rs] * (o_ref.shape[1] // rs.shape[1]), axis=1)
        g = g * rs
        o_ref[rows, :] = (g * jax.nn.sigmoid(g) * (u * rs)).astype(o_ref.dtype)


def _swiglu(h, row_scale, w_gate, w_up, *, tiles, name):
    m, k = h.shape
    n = w_gate.shape[1]
    tm, tn = tiles
    assert m % tm == 0 and n % tn == 0 and tn % row_scale.shape[1] == 0 and tm % ROW_CHUNKS == 0
    return pl.pallas_call(
        _swiglu_kernel,
        grid=(m // tm, n // tn),
        in_specs=[_row_spec(tm, k),
                  pl.BlockSpec((tm, row_scale.shape[1]), lambda i, j: (i, 0)),
                  pl.BlockSpec((k, tn), lambda i, j: (0, j)),
                  pl.BlockSpec((k, tn), lambda i, j: (0, j))],
        out_specs=pl.BlockSpec((tm, tn), lambda i, j: (i, j)),
        out_shape=jax.ShapeDtypeStruct((m, n), BF16),
        compiler_params=_cparams(("arbitrary", "arbitrary")),
        name=name,
    )(h, row_scale, w_gate, w_up)


def _head_rms(x, g):
    ms = jnp.mean(x * x, axis=-1, keepdims=True)
    return x * lax.rsqrt(ms + EPS) * g


def _rope(x, cos, sin_signed):
    lane = lax.broadcasted_iota(jnp.int32, x.shape, 1)
    first_half = (lane % AXIS_ROPE_DIM) < (AXIS_ROPE_DIM // 2)
    partner = jnp.where(first_half,
                        pltpu.roll(x, HEAD_DIM - AXIS_ROPE_DIM // 2, 1),
                        pltpu.roll(x, AXIS_ROPE_DIM // 2, 1))
    return x * cos + partner * sin_signed


def _rope_tables(seq):
    t = np.arange(seq)
    row, col = (t // GRID_W).astype(np.float64), (t % GRID_W).astype(np.float64)
    inv = ROPE_THETA ** (-np.arange(0, AXIS_ROPE_DIM, 2, dtype=np.float64) / AXIS_ROPE_DIM)
    ang_row, ang_col = row[:, None] * inv, col[:, None] * inv
    cr, sr, cc, sc = np.cos(ang_row), np.sin(ang_row), np.cos(ang_col), np.sin(ang_col)
    cos = np.concatenate([cr, cr, cc, cc], axis=-1).astype(np.float32)
    sin_signed = np.concatenate([-sr, sr, -sc, sc], axis=-1).astype(np.float32)
    return jnp.asarray(cos), jnp.asarray(sin_signed)


SOFTMAX_ROWS = 16
LOG2E = math.log2(math.e)
QPREP_ROWS = 64


def _attn_a_kernel(q_ref, k_ref, v_ref, cq_ref, sq_ref, ck_ref, sk_ref, gq_ref, gk_ref,
                   o_ref, kn_ref, v1_ref, s_ref, p_ref):
    n_kv = kn_ref.shape[0]
    n_heads = n_kv * N_A_GROUP

    @pl.when(pl.program_id(2) == 0)
    def _():
        for kv in range(n_kv):
            cols = slice(kv * HEAD_DIM, (kv + 1) * HEAD_DIM)
            k = _head_rms(k_ref[0, :, cols].astype(F32), gk_ref[...])
            kn_ref[kv] = _rope(k, ck_ref[...], sk_ref[...]).astype(BF16)
            v1_ref[kv, :, :HEAD_DIM] = v_ref[0, :, cols]
            v1_ref[kv, :, HEAD_DIM:] = jnp.ones((v_ref.shape[1], HEAD_DIM), BF16)

    tq = q_ref.shape[1]

    def scores(u):
        parts = []
        for r in range(tq // QPREP_ROWS):
            rows = pl.ds(r * QPREP_ROWS, QPREP_ROWS)
            q = _head_rms(q_ref[0, rows, u * HEAD_DIM:(u + 1) * HEAD_DIM].astype(F32),
                          gq_ref[...])
            q = _rope(q, cq_ref[rows, :], sq_ref[rows, :]) * (ATTN_SCALE * LOG2E)
            parts.append(q.astype(BF16))
        q = jnp.concatenate(parts, axis=0)
        s_ref[u % 2] = lax.dot_general(q, kn_ref[u // N_A_GROUP], (((1,), (1,)), ((), ())),
                                       preferred_element_type=F32)

    def softmax(u):
        slot = u % 2
        for r in range(tq // SOFTMAX_ROWS):
            rows = pl.ds(r * SOFTMAX_ROWS, SOFTMAX_ROWS)
            s = s_ref[slot, rows, :]
            p_ref[slot, rows, :] = jnp.exp2(s - jnp.max(s, axis=-1, keepdims=True)).astype(BF16)

    def weighted_values(u):
        ov = jnp.dot(p_ref[u % 2], v1_ref[u // N_A_GROUP],
                     preferred_element_type=F32)
        o_ref[0, :, u * HEAD_DIM:(u + 1) * HEAD_DIM] = (
            ov[:, :HEAD_DIM] / ov[:, HEAD_DIM:]).astype(o_ref.dtype)

    scores(0)
    for u in range(n_heads):
        if u + 1 < n_heads:
            scores(u + 1)
        softmax(u)
        weighted_values(u)


def _attention_a(qkv3, cos, sin_signed, gq, gk, *, tq):
    b, s, _ = qkv3.shape
    n_kv = ATTN_A_KV_PER_STEP
    qw, kvw = n_kv * HEAD_BLOCK, n_kv * HEAD_DIM
    kb = N_A_HEADS * HEAD_DIM // kvw
    vb = (N_A_HEADS + N_A_KV_HEADS) * HEAD_DIM // kvw
    assert N_A_KV_HEADS % n_kv == 0 and s % tq == 0
    return pl.pallas_call(
        _attn_a_kernel,
        grid=(b, N_A_KV_HEADS // n_kv, s // tq),
        in_specs=[pl.BlockSpec((1, tq, qw), lambda bi, kv, qi: (bi, qi, kv)),
                  pl.BlockSpec((1, s, kvw), lambda bi, kv, qi: (bi, 0, kb + kv)),
                  pl.BlockSpec((1, s, kvw), lambda bi, kv, qi: (bi, 0, vb + kv)),
                  pl.BlockSpec((tq, HEAD_DIM), lambda bi, kv, qi: (qi, 0)),
                  pl.BlockSpec((tq, HEAD_DIM), lambda bi, kv, qi: (qi, 0)),
                  pl.BlockSpec((s, HEAD_DIM), lambda bi, kv, qi: (0, 0)),
                  pl.BlockSpec((s, HEAD_DIM), lambda bi, kv, qi: (0, 0)),
                  pl.BlockSpec((1, HEAD_DIM), lambda bi, kv, qi: (0, 0)),
                  pl.BlockSpec((1, HEAD_DIM), lambda bi, kv, qi: (0, 0))],
        out_specs=pl.BlockSpec((1, tq, qw), lambda bi, kv, qi: (bi, qi, kv)),
        out_shape=jax.ShapeDtypeStruct((b, s, N_A_HEADS * HEAD_DIM), BF16),
        scratch_shapes=[pltpu.VMEM((n_kv, s, HEAD_DIM), BF16),
                        pltpu.VMEM((n_kv, s, 2 * HEAD_DIM), BF16),
                        pltpu.VMEM((2, tq, s), F32),
                        pltpu.VMEM((2, tq, s), BF16)],
        compiler_params=_cparams(("arbitrary", "arbitrary", "arbitrary")),
        name="attention_a",
    )(qkv3, qkv3, qkv3, cos, sin_signed, cos, sin_signed, gq, gk)


RADIUS = 64
assert all(w // (2 * d) == RADIUS for w, d in B_PATTERNS)
SCORES_AHEAD = 1


def _t5_bucket_np(rel):
    nb = REL_BUCKETS // 2
    max_exact = nb // 2
    side = np.where(rel > 0, nb, 0)
    n = np.abs(rel)
    nf = np.maximum(n, 1).astype(np.float32)
    large = max_exact + (np.log(nf / np.float32(max_exact))
                         / np.float32(math.log(REL_MAX_DIST / max_exact))
                         * np.float32(nb - max_exact)).astype(np.int32)
    large = np.minimum(large, nb - 1)
    return (side + np.where(n < max_exact, n, large)).astype(np.int32)


def _band_geometry(seq, dil):
    length = seq // dil
    kw = min(2 * Q_TILE, length)
    nblk = length // Q_TILE
    starts = [min(max(mb * Q_TILE - RADIUS, 0), length - kw) for mb in range(nblk)]
    shifts = sorted({st - mb * Q_TILE for mb, st in enumerate(starts)}, reverse=True)
    return length, kw, nblk, shifts


def _bucket_tiles(seq, dil):
    _, kw, _, shifts = _band_geometry(seq, dil)
    i = np.arange(Q_TILE)[:, None]
    j = np.arange(kw)[None, :]
    tiles = []
    for shift in shifts:
        rel = j - i + shift
        tiles.append(np.where(np.abs(rel) <= RADIUS, _t5_bucket_np(rel * dil), -1))
    return np.stack(tiles, axis=0).astype(np.int32)


def _attn_b_kernel(rb_ref, bk0_ref, bk1_ref, bk2_ref,
                   q0_ref, q1_ref, q2_ref, k0_ref, k1_ref, k2_ref, v0_ref, v1_ref, v2_ref,
                   gq_ref, gk_ref, o_ref,
                   qc_ref, kc_ref, vc_ref, on_ref, ln_ref, t0_ref, t1_ref, t2_ref, *, seq):
    head = pl.program_id(0)
    gq, gk = gq_ref[...], gk_ref[...]
    bk_refs = (bk0_ref, bk1_ref, bk2_ref)
    tile_refs = (t0_ref, t1_ref, t2_ref)
    q_refs, k_refs, v_refs = (q0_ref, q1_ref, q2_ref), (k0_ref, k1_ref, k2_ref), (v0_ref, v1_ref, v2_ref)

    @pl.when(pl.program_id(1) == 0)
    def _():
        for g in range(N_B_GROUPS):
            col = g * N_B_HEADS_PER_GROUP + head
            for var in range(bk_refs[g].shape[0]):
                bkt = bk_refs[g][var]

                def pick(b, acc, bkt=bkt, col=col):
                    return jnp.where(bkt == b, rb_ref[b, col], acc)

                tile_refs[g][var] = lax.fori_loop(0, REL_BUCKETS, pick,
                                                  jnp.full(bkt.shape, NEG_INF, F32))

    def prepare(g):
        dil = B_PATTERNS[g][1]
        length = seq // dil
        for c in range(dil):
            rows = pl.ds(c, length, stride=dil) if dil > 1 else pl.ds(0, length)
            dst = pl.ds(c * length, length)
            qc_ref[g, dst, :] = (_head_rms(q_refs[g][0, rows, :], gq) * ATTN_SCALE).astype(BF16)
            kc_ref[g, dst, :] = _head_rms(k_refs[g][0, rows, :], gk).astype(BF16)
            vc_ref[g, dst, :] = v_refs[g][0, rows, :].astype(BF16)

    def run_blocks(g):
        dil = B_PATTERNS[g][1]
        length, kw, nblk, shifts = _band_geometry(seq, dil)

        def key_rows(c, mb):
            start = min(max(mb * Q_TILE - RADIUS, 0), length - kw)
            return pl.ds(c * length + start, kw), shifts.index(start - mb * Q_TILE)

        def scores(c, mb):
            krows, var = key_rows(c, mb)
            q = qc_ref[g, pl.ds(c * length + mb * Q_TILE, Q_TILE), :]
            logits = lax.dot_general(q, kc_ref[g, krows, :], (((1,), (1,)), ((), ())),
                                     preferred_element_type=F32)
            return logits + tile_refs[g][var]

        def finish(c, mb, logits):
            krows, _ = key_rows(c, mb)
            m = jnp.max(logits, axis=-1, keepdims=True)
            p = jnp.exp(logits - m)
            denom = jnp.sum(p, axis=-1, keepdims=True)
            o = jnp.dot(p.astype(BF16), vc_ref[g, krows, :], preferred_element_type=F32) / denom
            t0 = c + dil * mb * Q_TILE
            rows = pl.ds(t0, Q_TILE, stride=dil) if dil > 1 else pl.ds(t0, Q_TILE)
            on_ref[g, rows, :] = o
            ln_ref[g, rows, :] = jnp.broadcast_to(m + jnp.log(denom), (Q_TILE, HEAD_DIM))

        blocks = [(c, mb) for c in range(dil) for mb in range(nblk)]
        ahead = [scores(*blk) for blk in blocks[:SCORES_AHEAD]]
        for i, blk in enumerate(blocks):
            if i + SCORES_AHEAD < len(blocks):
                ahead.append(scores(*blocks[i + SCORES_AHEAD]))
            finish(*blk, ahead.pop(0))

    for g in range(N_B_GROUPS):
        prepare(g)
        run_blocks(g)

    chunk = 2 * Q_TILE

    def combine(i, carry):
        rows = pl.ds(pl.multiple_of(i * chunk, chunk), chunk)
        l0, l1, l2 = ln_ref[0, rows, :], ln_ref[1, rows, :], ln_ref[2, rows, :]
        mx = jnp.maximum(jnp.maximum(l0, l1), l2)
        e0, e1, e2 = jnp.exp(l0 - mx), jnp.exp(l1 - mx), jnp.exp(l2 - mx)
        num = e0 * on_ref[0, rows, :] + e1 * on_ref[1, rows, :] + e2 * on_ref[2, rows, :]
        o_ref[0, rows, :] = (num / (e0 + e1 + e2)).astype(o_ref.dtype)
        return carry

    lax.fori_loop(0, seq // chunk, combine, 0)


def _attention_b(qkv3, rel_bias, gq, gk):
    b, s, _ = qkv3.shape
    nh = N_B_HEADS_PER_GROUP
    bks = [jnp.asarray(_bucket_tiles(s, dil)) for _, dil in B_PATTERNS]

    def head_spec(part, g):
        base = (part * N_B_GROUPS + g) * nh
        return pl.BlockSpec((1, s, HEAD_DIM), lambda h, bi: (bi, 0, base + h))

    def full_spec(arr):
        nd = arr.ndim
        return pl.BlockSpec(arr.shape, lambda h, bi: (0,) * nd)

    in_specs = [pl.BlockSpec(memory_space=pltpu.SMEM)]
    in_specs += [full_spec(bk) for bk in bks]
    in_specs += [head_spec(part, g) for part in range(3) for g in range(N_B_GROUPS)]
    in_specs += [pl.BlockSpec((1, HEAD_DIM), lambda h, bi: (0, 0))] * 2
    scratch = [pltpu.VMEM((N_B_GROUPS, s, HEAD_DIM), BF16)] * 3
    scratch += [pltpu.VMEM((N_B_GROUPS, s, HEAD_DIM), F32)] * 2
    scratch += [pltpu.VMEM(bk.shape, F32) for bk in bks]
    return pl.pallas_call(
        functools.partial(_attn_b_kernel, seq=s),
        grid=(nh, b),
        in_specs=in_specs,
        out_specs=pl.BlockSpec((1, s, HEAD_DIM), lambda h, bi: (bi, 0, h)),
        out_shape=jax.ShapeDtypeStruct((b, s, nh * HEAD_DIM), BF16),
        scratch_shapes=scratch,
        compiler_params=_cparams(("arbitrary", "arbitrary")),
        name="attention_b",
    )(rel_bias, *bks, *([qkv3] * 9), gq, gk)


def kernel(x, norm1_g, w_in, b_gate, q_norm_a, k_norm_a, q_norm_b, k_norm_b, rel_bias,
           w_proj_a, w_proj_b, w_out, norm2_g, w_ffn_gate, w_ffn_up, w_ffn_down):
    b, s, d = x.shape
    m = b * s
    a_w = (N_A_HEADS + 2 * N_A_KV_HEADS) * HEAD_DIM
    b_w = 3 * N_B_GROUPS * N_B_HEADS_PER_GROUP * HEAD_DIM
    cos, sin_signed = _rope_tables(s)

    xf = x.reshape(m, d)
    for l in range(norm1_g.shape[0]):
        h = _rmsnorm(xf, norm1_g[l], name="rmsnorm1")
        qkv_a_gates = _matmul(h, w_in[l], col_ranges=((0, a_w), (a_w + b_w, 2 * d)),
                              tiles=TILES["in_proj"], out_dtype=BF16, name="in_proj_a_gates")
        qkv_b = _matmul(h, w_in[l], col_ranges=((a_w, b_w),), tiles=TILES["in_proj"],
                        out_dtype=F32, name="in_proj_b")

        o_a = _attention_a(qkv_a_gates.reshape(b, s, -1), cos, sin_signed,
                           q_norm_a[l].reshape(1, -1), k_norm_a[l].reshape(1, -1),
                           tq=ATTN_A_Q_TILE)
        o_b = _attention_b(qkv_b.reshape(b, s, b_w), rel_bias, q_norm_b[l].reshape(1, -1),
                           k_norm_b[l].reshape(1, -1))

        merged = _merge(o_a.reshape(m, -1), o_b.reshape(m, -1), qkv_a_gates, b_gate[l],
                        w_proj_a[l], w_proj_b[l], gate_col=a_w, tiles=TILES["gated_merge"],
                        name="gated_merge")
        xf, xg, rs = _matmul_resid_norm(merged, w_out[l], xf, norm2_g[l],
                                        tiles=TILES["out_proj"], name="out_proj")
        act = _swiglu(xg, rs, w_ffn_gate[l], w_ffn_up[l], tiles=TILES["swiglu_up"],
                      name="swiglu_up")
        xf = _matmul_resid_rows(act, w_ffn_down[l], xf, tiles=TILES["ffn_down"], name="ffn_down")
    return xf.reshape(b, s, d)
```
